```python
import math
import jax, jax.numpy as jnp
from jax import lax
import numpy as np

D_MODEL = 4096
BATCH = 2
SEQ = 4096
DEPTH = 2

DIFF_QK_DIM = 64
DIFF_V_DIM = 2 * DIFF_QK_DIM
N_DIFF_HEADS = D_MODEL // (2 * DIFF_V_DIM)
MOBA_HEAD_DIM = 128
N_MOBA_HEADS = D_MODEL // (2 * MOBA_HEAD_DIM)
MOBA_BLOCK = 256
MOBA_TOPK = 3
MOBA_Q_CHUNK = 32
DENSE_Q_BLOCK = 128
A_QK_W = N_DIFF_HEADS * 2 * DIFF_QK_DIM
A_V_W = N_DIFF_HEADS * DIFF_V_DIM
B_W = N_MOBA_HEADS * MOBA_HEAD_DIM
EVEN_SPLITS = [int(v) for v in np.cumsum([A_QK_W, A_QK_W, A_V_W, B_W, B_W])]

SWA_HEAD_DIM = 64
N_SWA_HEADS = D_MODEL // SWA_HEAD_DIM
N_SWA_KV_HEADS = N_SWA_HEADS // 8
SWA_WINDOW = 128
ODD_SPLITS = [N_SWA_HEADS * SWA_HEAD_DIM, (N_SWA_HEADS + N_SWA_KV_HEADS) * SWA_HEAD_DIM]

N_EXPERTS = 32
TOP_K = 4
D_EXPERT = 1024
SWIGLU_LIMIT = 7.0
SWIGLU_ALPHA = 1.702

LN_EPS = 1e-5
SUBLN_EPS = 1e-5
DEEPNORM_ALPHA = (2.0 * DEPTH) ** 0.25
DEEPNORM_BETA = (8.0 * DEPTH) ** -0.25

kernel_name = "hybrid_diff_moba_swa_moe_deepnorm"


def alibi_slopes(n_heads):
    return 2.0 ** (-8.0 * (jnp.arange(n_heads, dtype=jnp.float32) + 1.0) / n_heads)


def layer_norm(x, g, b):
    xf = x.astype(jnp.float32)
    mu = jnp.mean(xf, axis=-1, keepdims=True)
    var = jnp.mean(jnp.square(xf - mu), axis=-1, keepdims=True)
    y = (xf - mu) * lax.rsqrt(var + LN_EPS) * g.astype(jnp.float32) + b.astype(jnp.float32)
    return y.astype(x.dtype)


def post_norm(x, h, g, b):
    return layer_norm(DEEPNORM_ALPHA * x + h, g, b)


def diff_attention(q, k, v, lam, lam_init, subln_g, slopes):
    B_, H, _, S_, dq = q.shape
    nq = S_ // DENSE_Q_BLOCK
    qb = q.reshape(B_, H, 2, nq, DENSE_Q_BLOCK, dq).transpose(3, 0, 1, 2, 4, 5)
    kpos = jnp.arange(S_)
    scale = DIFF_QK_DIM ** -0.5

    def one_block(args):
        qc, blk = args
        qpos = blk * DENSE_Q_BLOCK + jnp.arange(DENSE_Q_BLOCK)
        dist = (qpos[:, None] - kpos[None, :]).astype(jnp.float32)
        bias = -slopes[:, None, None] * dist
        s = jnp.einsum('bhmqd,bhmkd->bhmqk', qc, k).astype(jnp.float32) * scale + bias[None, :, None]
        s = jnp.where(dist >= 0, s, -jnp.inf)
        p = jax.nn.softmax(s, axis=-1)
        a = p[:, :, 0] - lam * p[:, :, 1]
        return jnp.einsum('bhqk,bhkd->bhqd', a.astype(v.dtype), v)

    o = lax.map(one_block, (qb, jnp.arange(nq)))
    o = o.transpose(1, 0, 3, 2, 4).reshape(B_, S_, H, DIFF_V_DIM)
    of = o.astype(jnp.float32)
    of = of * lax.rsqrt(jnp.mean(jnp.square(of), axis=-1, keepdims=True) + SUBLN_EPS)
    of = of * subln_g.astype(jnp.float32) * (1.0 - lam_init)
    return of.astype(v.dtype).reshape(B_, S_, H * DIFF_V_DIM)


def moba_attention(q, k, v, slopes):
    B_, H, S_, Dh = q.shape
    nb = -(-S_ // MOBA_BLOCK)
    Sp = nb * MOBA_BLOCK
    pad = ((0, 0), (0, 0), (0, Sp - S_), (0, 0))
    q, k, v = jnp.pad(q, pad), jnp.pad(k, pad), jnp.pad(v, pad)
    kb = k.reshape(B_, H, nb, MOBA_BLOCK, Dh)
    vb = v.reshape(B_, H, nb, MOBA_BLOCK, Dh)
    k_mean = jnp.mean(kb.astype(jnp.float32), axis=3)
    gate = jnp.einsum('bhsd,bhnd->bhsn', q.astype(jnp.float32), k_mean)
    qblk = jnp.arange(Sp) // MOBA_BLOCK
    past = jnp.arange(nb)[None, :] < qblk[:, None]
    gate = jnp.where(past, gate, -jnp.inf)
    k_sel = min(MOBA_TOPK, nb)
    _, sel = lax.top_k(gate, k_sel)
    own = jnp.broadcast_to(qblk[None, None, :, None], (B_, H, Sp, 1))
    idx = jnp.concatenate([sel, own], axis=-1)
    valid = jnp.concatenate([sel < qblk[None, None, :, None], jnp.ones_like(own, dtype=bool)], axis=-1)
    ns = k_sel + 1
    nc = Sp // MOBA_Q_CHUNK
    qc = q.reshape(B_, H, nc, MOBA_Q_CHUNK, Dh).transpose(2, 0, 1, 3, 4)
    idx_c = idx.reshape(B_, H, nc, MOBA_Q_CHUNK, ns).transpose(2, 0, 1, 3, 4)
    val_c = valid.reshape(B_, H, nc, MOBA_Q_CHUNK, ns).transpose(2, 0, 1, 3, 4)
    bi = jnp.arange(B_)[:, None, None, None]
    hi = jnp.arange(H)[None, :, None, None]
    offs = jnp.arange(MOBA_BLOCK)
    scale = Dh ** -0.5

    def one_chunk(args):
        qx, ix, vx, c = args
        kg = kb[bi, hi, ix]
        vg = vb[bi, hi, ix]
        qpos = c * MOBA_Q_CHUNK + jnp.arange(MOBA_Q_CHUNK)
        kpos = ix[..., None] * MOBA_BLOCK + offs
        dist = (qpos[None, None, :, None, None] - kpos).astype(jnp.float32)
        ok = vx[..., None] & (dist >= 0)
        s = jnp.einsum('bhqd,bhqnkd->bhqnk', qx, kg).astype(jnp.float32) * scale
        s = s - slopes[None, :, None, None, None] * dist
        s = jnp.where(ok, s, -jnp.inf)
        p = jax.nn.softmax(s.reshape(B_, H, MOBA_Q_CHUNK, -1), axis=-1).reshape(s.shape)
        return jnp.einsum('bhqnk,bhqnkd->bhqd', p.astype(vg.dtype), vg)

    o = lax.map(one_chunk, (qc, idx_c, val_c, jnp.arange(nc)))
    o = o.transpose(1, 0, 3, 2, 4).reshape(B_, Sp, H * Dh)
    return o[:, :S_]


def swa_sink_attention(q, k, v, sinks, slopes):
    B_, S_, Hq, Dh = q.shape
    Hkv = k.shape[2]
    G = Hq // Hkv
    W = SWA_WINDOW
    nb = S_ // W
    qb = q.reshape(B_, nb, W, Hkv, G, Dh)
    kb = k.reshape(B_, nb, W, Hkv, Dh)
    vb = v.reshape(B_, nb, W, Hkv, Dh)
    prev = lambda t: jnp.pad(t, ((0, 0), (1, 0), (0, 0), (0, 0), (0, 0)))[:, :-1]
    kcat = jnp.concatenate([prev(kb), kb], axis=2)
    vcat = jnp.concatenate([prev(vb), vb], axis=2)
    qpos = jnp.arange(nb)[:, None] * W + jnp.arange(W)[None, :]
    kpos = (jnp.arange(nb)[:, None] - 1) * W + jnp.arange(2 * W)[None, :]
    dist_i = qpos[:, :, None] - kpos[:, None, :]
    ok = (dist_i >= 0) & (dist_i < W) & (kpos[:, None, :] >= 0)
    dist = dist_i.astype(jnp.float32)
    s = jnp.einsum('bnqhgd,bnkhd->bnhgqk', qb, kcat).astype(jnp.float32) * (Dh ** -0.5)
    s = s - slopes.reshape(Hkv, G)[None, None, :, :, None, None] * dist[None, :, None, None]
    s = jnp.where(ok[None, :, None, None], s, -jnp.inf)
    sink = sinks.astype(jnp.float32).reshape(Hkv, G)[None, None, :, :, None, None]
    m = jnp.maximum(jnp.max(s, axis=-1, keepdims=True), sink)
    e = jnp.exp(s - m)
    p = e / (jnp.sum(e, axis=-1, keepdims=True) + jnp.exp(sink - m))
    o = jnp.einsum('bnhgqk,bnkhd->bnqhgd', p.astype(vcat.dtype), vcat)
    return o.reshape(B_, S_, Hq * Dh)


def diff_moba_mixer(x, w_in, lq1, lk1, lq2, lk2, subln_g, w_o, layer_idx):
    B_, S_, _ = x.shape
    proj = x @ w_in
    a_q, a_k, a_v, b_q, b_k, b_v = jnp.split(proj, EVEN_SPLITS, axis=-1)
    to_diff = lambda t: t.reshape(B_, S_, N_DIFF_HEADS, 2, DIFF_QK_DIM).transpose(0, 2, 3, 1, 4)
    to_heads = lambda t, h, d: t.reshape(B_, S_, h, d).transpose(0, 2, 1, 3)
    lam_init = 0.8 - 0.6 * math.exp(-0.3 * layer_idx)
    lam = (jnp.exp(jnp.sum(lq1.astype(jnp.float32) * lk1.astype(jnp.float32)))
           - jnp.exp(jnp.sum(lq2.astype(jnp.float32) * lk2.astype(jnp.float32))) + lam_init)
    slopes = alibi_slopes(N_DIFF_HEADS + N_MOBA_HEADS)
    out_a = diff_attention(to_diff(a_q), to_diff(a_k), to_heads(a_v, N_DIFF_HEADS, DIFF_V_DIM),
                           lam, lam_init, subln_g, slopes[0::2])
    out_b = moba_attention(to_heads(b_q, N_MOBA_HEADS, MOBA_HEAD_DIM),
                           to_heads(b_k, N_MOBA_HEADS, MOBA_HEAD_DIM),
                           to_heads(b_v, N_MOBA_HEADS, MOBA_HEAD_DIM), slopes[1::2])
    return jnp.concatenate([out_a, out_b], axis=-1) @ w_o


def swa_mixer(x, w_in, sinks, w_o):
    B_, S_, _ = x.shape
    q, k, v = jnp.split(x @ w_in, ODD_SPLITS, axis=-1)
    q = q.reshape(B_, S_, N_SWA_HEADS, SWA_HEAD_DIM)
    k = k.reshape(B_, S_, N_SWA_KV_HEADS, SWA_HEAD_DIM)
    v = v.reshape(B_, S_, N_SWA_KV_HEADS, SWA_HEAD_DIM)
    return swa_sink_attention(q, k, v, sinks, alibi_slopes(N_SWA_HEADS)) @ w_o


def moe_ffn(x, router_w, router_b, w_gate, b_gate, w_up, b_up, w_down, b_down):
    B_, S_, D = x.shape
    xt = x.reshape(-1, D)
    logits = (xt @ router_w).astype(jnp.float32) + router_b.astype(jnp.float32)
    top_v, top_i = lax.top_k(logits, TOP_K)
    top_w = jax.nn.softmax(top_v, axis=-1)
    combine = jnp.sum(jax.nn.one_hot(top_i, N_EXPERTS, dtype=jnp.float32) * top_w[..., None], axis=1)
    combine = combine.astype(x.dtype)
    y = jnp.zeros_like(xt)
    for e in range(N_EXPERTS):
        gate = jnp.minimum(xt @ w_gate[e] + b_gate[e], SWIGLU_LIMIT)
        up = jnp.clip(xt @ w_up[e] + b_up[e], -SWIGLU_LIMIT, SWIGLU_LIMIT)
        h = (up + 1.0) * gate * jax.nn.sigmoid(SWIGLU_ALPHA * gate)
        y = y + combine[:, e:e + 1] * (h @ w_down[e] + b_down[e])
    return y.reshape(B_, S_, D)


def setup_inputs(seed: int = 0) -> dict:
    key = jax.random.key(seed)
    ks = list(jax.random.split(key, 64))
    cnt = [0]

    def nrm(shape, scale):
        k = ks[cnt[0]]
        cnt[0] += 1
        return jax.random.normal(k, shape, jnp.float32) * scale

    d = D_MODEL
    s_in = d ** -0.5
    beta = DEEPNORM_BETA

    def moe_params():
        return (nrm((d, N_EXPERTS), s_in), nrm((N_EXPERTS,), 0.01),
                nrm((N_EXPERTS, d, D_EXPERT), s_in), nrm((N_EXPERTS, D_EXPERT), 0.02),
                nrm((N_EXPERTS, d, D_EXPERT), s_in), nrm((N_EXPERTS, D_EXPERT), 0.02),
                nrm((N_EXPERTS, D_EXPERT, d), D_EXPERT ** -0.5 * beta), nrm((N_EXPERTS, d), 0.02))

    def ln_params():
        return 1.0 + nrm((d,), 0.02), nrm((d,), 0.02)

    x = nrm((BATCH, SEQ, d), 1.0)
    l0_w_in = jnp.concatenate([nrm((d, A_QK_W), s_in), nrm((d, A_QK_W), s_in), nrm((d, A_V_W), s_in * beta),
                               nrm((d, B_W), s_in), nrm((d, B_W), s_in), nrm((d, B_W), s_in * beta)], axis=1)
    l0_lq1, l0_lk1, l0_lq2, l0_lk2 = [nrm((DIFF_QK_DIM,), 0.1) for _ in range(4)]
    l0_subln_g = 1.0 + nrm((DIFF_V_DIM,), 0.02)
    l0_w_o = nrm((A_V_W + B_W, d), (A_V_W + B_W) ** -0.5 * beta)
    l0_ln1_g, l0_ln1_b = ln_params()
    l0_moe = moe_params()
    l0_ln2_g, l0_ln2_b = ln_params()
    kv_w = N_SWA_KV_HEADS * SWA_HEAD_DIM
    q_w = N_SWA_HEADS * SWA_HEAD_DIM
    l1_w_in = jnp.concatenate([nrm((d, q_w), s_in), nrm((d, kv_w), s_in), nrm((d, kv_w), s_in * beta)], axis=1)
    l1_sinks = nrm((N_SWA_HEADS,), 0.5)
    l1_w_o = nrm((q_w, d), q_w ** -0.5 * beta)
    l1_ln1_g, l1_ln1_b = ln_params()
    l1_moe = moe_params()
    l1_ln2_g, l1_ln2_b = ln_params()
    return {
        "x": x,
        "l0_w_in": l0_w_in, "l0_lambda_q1": l0_lq1, "l0_lambda_k1": l0_lk1,
        "l0_lambda_q2": l0_lq2, "l0_lambda_k2": l0_lk2, "l0_subln_g": l0_subln_g, "l0_w_o": l0_w_o,
        "l0_ln1_g": l0_ln1_g, "l0_ln1_b": l0_ln1_b,
        "l0_router_w": l0_moe[0], "l0_router_b": l0_moe[1], "l0_w_gate": l0_moe[2], "l0_b_gate": l0_moe[3],
        "l0_w_up": l0_moe[4], "l0_b_up": l0_moe[5], "l0_w_down": l0_moe[6], "l0_b_down": l0_moe[7],
        "l0_ln2_g": l0_ln2_g, "l0_ln2_b": l0_ln2_b,
        "l1_w_in": l1_w_in, "l1_sinks": l1_sinks, "l1_w_o": l1_w_o,
        "l1_ln1_g": l1_ln1_g, "l1_ln1_b": l1_ln1_b,
        "l1_router_w": l1_moe[0], "l1_router_b": l1_moe[1], "l1_w_gate": l1_moe[2], "l1_b_gate": l1_moe[3],
        "l1_w_up": l1_moe[4], "l1_b_up": l1_moe[5], "l1_w_down": l1_moe[6], "l1_b_down": l1_moe[7],
        "l1_ln2_g": l1_ln2_g, "l1_ln2_b": l1_ln2_b,
    }


def reference(x, l0_w_in, l0_lambda_q1, l0_lambda_k1, l0_lambda_q2, l0_lambda_k2, l0_subln_g, l0_w_o,
              l0_ln1_g, l0_ln1_b, l0_router_w, l0_router_b, l0_w_gate, l0_b_gate, l0_w_up, l0_b_up,
              l0_w_down, l0_b_down, l0_ln2_g, l0_ln2_b,
              l1_w_in, l1_sinks, l1_w_o, l1_ln1_g, l1_ln1_b, l1_router_w, l1_router_b, l1_w_gate,
              l1_b_gate, l1_w_up, l1_b_up, l1_w_down, l1_b_down, l1_ln2_g, l1_ln2_b):
    mixer_args = (
        (l0_w_in, l0_lambda_q1, l0_lambda_k1, l0_lambda_q2, l0_lambda_k2, l0_subln_g, l0_w_o),
        (l1_w_in, l1_sinks, l1_w_o),
    )
    norm_args = (
        (l0_ln1_g, l0_ln1_b, l0_ln2_g, l0_ln2_b),
        (l1_ln1_g, l1_ln1_b, l1_ln2_g, l1_ln2_b),
    )
    moe_args = (
        (l0_router_w, l0_router_b, l0_w_gate, l0_b_gate, l0_w_up, l0_b_up, l0_w_down, l0_b_down),
        (l1_router_w, l1_router_b, l1_w_gate, l1_b_gate, l1_w_up, l1_b_up, l1_w_down, l1_b_down),
    )
    for layer in range(DEPTH):
        if layer % 2 == 0:
            h = diff_moba_mixer(x, *mixer_args[layer], layer_idx=layer)
        else:
            h = swa_mixer(x, *mixer_args[layer])
        g1, b1, g2, b2 = norm_args[layer]
        x = post_norm(x, h, g1, b1)
        x = post_norm(x, moe_ffn(x, *moe_args[layer]), g2, b2)
    return x
```

```python
import functools
import math

import jax
import jax.numpy as jnp
from jax import lax
from jax.experimental import pallas as pl
from jax.experimental.pallas import tpu as pltpu

F32 = jnp.float32
BF16 = jnp.bfloat16
NEG_INF = float("-inf")

DEPTH = 2
DIFF_QK_DIM = 64
N_DIFF_HEADS = 16
N_MOBA_HEADS = 16
MOBA_BLOCK = 256
MOBA_TOPK = 3
N_SWA_HEADS = 64
N_SWA_KV_HEADS = 8
SWA_HEAD_DIM = 64
SWA_WINDOW = 128
N_EXPERTS = 32
TOP_K = 4
SWIGLU_LIMIT = 7.0
SWIGLU_ALPHA = 1.702
LN_EPS = 1e-5
SUBLN_EPS = 1e-5
DEEPNORM_ALPHA = (2.0 * DEPTH) ** 0.25

LANES = 128
VMEM_LIMIT = 56 * 1024 * 1024

MM_TM, MM_TN = 512, 1024
DIFF_T = 512
SWA_T = SWA_WINDOW
NORM_TM = 256
MOE_TM = 256
MOE_TF = 512
MOE_TN = 2048
GATHER_T = 256
COMBINE_T = 128


def _params(*sem):
    return pltpu.CompilerParams(dimension_semantics=sem, vmem_limit_bytes=VMEM_LIMIT)


def _alibi_slopes(n):
    return 2.0 ** (-8.0 * (jnp.arange(n, dtype=F32) + 1.0) / n)


def _nt_dot(a, b):
    return lax.dot_general(a, b, (((1,), (1,)), ((), ())), preferred_element_type=F32)


def _matmul_kernel(x_ref, w_ref, o_ref):
    o_ref[...] = jnp.dot(x_ref[...], w_ref[...], preferred_element_type=F32).astype(o_ref.dtype)


def _matmul(x, w, out_dtype, tm=MM_TM, tn=MM_TN):
    m, k = x.shape
    n = w.shape[1]
    tm, tn = min(tm, m), min(tn, n)
    return pl.pallas_call(
        _matmul_kernel,
        grid=(n // tn, m // tm),
        in_specs=[pl.BlockSpec((tm, k), lambda j, i: (i, 0)),
                  pl.BlockSpec((k, tn), lambda j, i: (0, j))],
        out_specs=pl.BlockSpec((tm, tn), lambda j, i: (i, j)),
        out_shape=jax.ShapeDtypeStruct((m, n), out_dtype),
        name="matmul",
        compiler_params=_params("parallel", "parallel"),
    )(x, w)


def _matmul2_kernel(xa_ref, xb_ref, w_ref, o_ref, *, ka):
    acc = jnp.dot(xa_ref[...], w_ref[:ka, :], preferred_element_type=F32)
    acc = acc + jnp.dot(xb_ref[...], w_ref[ka:, :], preferred_element_type=F32)
    o_ref[...] = acc.astype(o_ref.dtype)


def _matmul2(xa, xb, w, out_dtype, tm=MM_TM, tn=MM_TN):
    m, ka = xa.shape
    kb = xb.shape[1]
    n = w.shape[1]
    tm, tn = min(tm, m), min(tn, n)
    return pl.pallas_call(
        functools.partial(_matmul2_kernel, ka=ka),
        grid=(n // tn, m // tm),
        in_specs=[pl.BlockSpec((tm, ka), lambda j, i: (i, 0)),
                  pl.BlockSpec((tm, kb), lambda j, i: (i, 0)),
                  pl.BlockSpec((ka + kb, tn), lambda j, i: (0, j))],
        out_specs=pl.BlockSpec((tm, tn), lambda j, i: (i, j)),
        out_shape=jax.ShapeDtypeStruct((m, n), out_dtype),
        name="matmul2",
        compiler_params=_params("parallel", "parallel"),
    )(xa, xb, w)


def _softmax_step(m_ref, l_ref, a_ref, s, v):
    m_prev = m_ref[...]
    m_next = jnp.maximum(m_prev, jnp.max(s, axis=1, keepdims=True))
    alpha = jnp.exp(m_prev - m_next)
    p = jnp.exp(s - m_next)
    l_ref[...] = alpha * l_ref[...] + jnp.sum(p, axis=1, keepdims=True)
    a_ref[...] = alpha * a_ref[...] + jnp.dot(p.astype(BF16), v, preferred_element_type=F32)
    m_ref[...] = m_next


def _softmax_init(m_ref, l_ref, a_ref):
    m_ref[...] = jnp.full(m_ref.shape, NEG_INF, F32)
    l_ref[...] = jnp.zeros(l_ref.shape, F32)
    a_ref[...] = jnp.zeros(a_ref.shape, F32)


def _diff_attn_kernel(slopes_ref, lq1_ref, lk1_ref, lq2_ref, lk2_ref, g_ref, q_ref, k_ref, v_ref, o_ref,
                      m1, l1, a1, m2, l2, a2, *, t, lam_init):
    h = pl.program_id(1)
    i = pl.program_id(2)
    slope = slopes_ref[h]
    lam = (jnp.exp(jnp.sum(lq1_ref[...] * lk1_ref[...], keepdims=True))
           - jnp.exp(jnp.sum(lq2_ref[...] * lk2_ref[...], keepdims=True)) + lam_init)

    q = q_ref[...] * jnp.asarray(DIFF_QK_DIM ** -0.5, BF16)
    lane = lax.broadcasted_iota(jnp.int32, q.shape, 1)
    q1 = jnp.where(lane < DIFF_QK_DIM, q, jnp.zeros_like(q))
    q2 = jnp.where(lane >= DIFF_QK_DIM, q, jnp.zeros_like(q))

    rel = lax.broadcasted_iota(jnp.int32, (t, t), 0) - lax.broadcasted_iota(jnp.int32, (t, t), 1)
    bias0 = -slope * rel.astype(F32)

    _softmax_init(m1, l1, a1)
    _softmax_init(m2, l2, a2)

    def tile(j, bias, causal):
        start = pl.multiple_of(j * t, t)
        kt = k_ref[pl.ds(start, t), :]
        vt = v_ref[pl.ds(start, t), :]
        for qz, (m_r, l_r, a_r) in ((q1, (m1, l1, a1)), (q2, (m2, l2, a2))):
            s = _nt_dot(qz, kt) + bias
            if causal:
                s = jnp.where(rel >= 0, s, NEG_INF)
            _softmax_step(m_r, l_r, a_r, s, vt)

    tile(i, bias0, True)

    def body(j, carry):
        tile(j, bias0 - slope * ((i - j) * t).astype(F32), False)
        return carry

    lax.fori_loop(0, i, body, 0)

    o = a1[...] / l1[...] - lam * (a2[...] / l2[...])
    o = o * lax.rsqrt(jnp.mean(o * o, axis=-1, keepdims=True) + SUBLN_EPS)
    o = o * g_ref[...] * (1.0 - lam_init)
    o_ref[...] = o.astype(o_ref.dtype)


def _diff_attention(proj, lq1, lk1, lq2, lk2, subln_g, slopes, batch, seq, lam_init):
    t = min(DIFF_T, seq)
    nq = seq // t
    nh = N_DIFF_HEADS
    vec = lambda n: pl.BlockSpec((1, n), lambda b, h, i: (0, 0))
    kern = functools.partial(_diff_attn_kernel, t=t, lam_init=lam_init)
    stat = pltpu.VMEM((t, 1), F32)
    acc = pltpu.VMEM((t, LANES), F32)
    return pl.pallas_call(
        kern,
        grid=(batch, nh, nq),
        in_specs=[pl.BlockSpec(memory_space=pltpu.SMEM),
                  vec(DIFF_QK_DIM), vec(DIFF_QK_DIM), vec(DIFF_QK_DIM), vec(DIFF_QK_DIM), vec(LANES),
                  pl.BlockSpec((t, LANES), lambda b, h, i: (b * nq + i, h)),
                  pl.BlockSpec((seq, LANES), lambda b, h, i: (b, nh + h)),
                  pl.BlockSpec((seq, LANES), lambda b, h, i: (b, 2 * nh + h))],
        out_specs=pl.BlockSpec((t, LANES), lambda b, h, i: (b * nq + i, h)),
        out_shape=jax.ShapeDtypeStruct((batch * seq, nh * LANES), BF16),
        scratch_shapes=[stat, stat, acc, stat, stat, acc],
        name="diff_attention",
        compiler_params=_params("parallel", "parallel", "parallel"),
    )(slopes, lq1.reshape(1, -1), lk1.reshape(1, -1), lq2.reshape(1, -1), lk2.reshape(1, -1),
      subln_g.reshape(1, -1), proj, proj, proj)


def _moba_kernel(slopes_ref, q_ref, k_ref, v_ref, o_ref, km_hi, km_mid, km_lo, selb, m_r, l_r, a_r, *, nb):
    blk = MOBA_BLOCK
    h = pl.program_id(1)
    i = pl.program_id(2)
    slope = slopes_ref[h]

    @pl.when(i == 0)
    def _():
        km = jnp.sum(k_ref[...].astype(F32).reshape(nb, blk, LANES), axis=1) * (1.0 / blk)
        hi = km.astype(BF16)
        r1 = km - hi.astype(F32)
        mid = r1.astype(BF16)
        km_hi[...] = hi
        km_mid[...] = mid
        km_lo[...] = (r1 - mid.astype(F32)).astype(BF16)

    q = q_ref[...]
    gate = _nt_dot(q, km_hi[...]) + _nt_dot(q, km_mid[...]) + _nt_dot(q, km_lo[...])
    blk_id = lax.broadcasted_iota(jnp.int32, gate.shape, 1)
    past = blk_id < i
    gate = jnp.where(past, gate, NEG_INF)
    beaten = jnp.zeros(gate.shape, jnp.int32)
    for n in range(nb):
        gn = gate[:, n:n + 1]
        wins = jnp.where(gn > gate, 1, jnp.where(gn == gate, jnp.where(blk_id > n, 1, 0), 0))
        beaten = beaten + wins
    chosen = jnp.where(past, jnp.where(beaten < MOBA_TOPK, 0.0, NEG_INF), NEG_INF)
    for n in range(nb):
        selb[n] = jnp.broadcast_to(chosen[:, n:n + 1], (blk, LANES))

    rel = lax.broadcasted_iota(jnp.int32, (blk, blk), 0) - lax.broadcasted_iota(jnp.int32, (blk, blk), 1)
    bias0 = -slope * rel.astype(F32)
    scale = LANES ** -0.5
    _softmax_init(m_r, l_r, a_r)

    start = pl.multiple_of(i * blk, blk)
    s = _nt_dot(q, k_ref[pl.ds(start, blk), :]) * scale + bias0
    _softmax_step(m_r, l_r, a_r, jnp.where(rel >= 0, s, NEG_INF), v_ref[pl.ds(start, blk), :])

    def body(j, carry):
        st = pl.multiple_of(j * blk, blk)
        sb = selb[j]
        sj = (_nt_dot(q, k_ref[pl.ds(st, blk), :]) * scale + (bias0 - slope * ((i - j) * blk).astype(F32))
              + jnp.concatenate([sb] * (blk // LANES), axis=1))
        _softmax_step(m_r, l_r, a_r, sj, v_ref[pl.ds(st, blk), :])
        return carry

    lax.fori_loop(0, i, body, 0)
    o_ref[...] = (a_r[...] / l_r[...]).astype(o_ref.dtype)


def _moba_attention(proj, slopes, batch, seq):
    blk = MOBA_BLOCK
    nb = seq // blk
    nh = N_MOBA_HEADS
    base = 3 * N_DIFF_HEADS
    kern = functools.partial(_moba_kernel, nb=nb)
    return pl.pallas_call(
        kern,
        grid=(batch, nh, nb),
        in_specs=[pl.BlockSpec(memory_space=pltpu.SMEM),
                  pl.BlockSpec((blk, LANES), lambda b, h, i: (b * nb + i, base + h)),
                  pl.BlockSpec((seq, LANES), lambda b, h, i: (b, base + nh + h)),
                  pl.BlockSpec((seq, LANES), lambda b, h, i: (b, base + 2 * nh + h))],
        out_specs=pl.BlockSpec((blk, LANES), lambda b, h, i: (b * nb + i, h)),
        out_shape=jax.ShapeDtypeStruct((batch * seq, nh * LANES), BF16),
        scratch_shapes=[pltpu.VMEM((nb, LANES), BF16), pltpu.VMEM((nb, LANES), BF16), pltpu.VMEM((nb, LANES), BF16),
                        pltpu.VMEM((nb, blk, LANES), F32),
                        pltpu.VMEM((blk, 1), F32), pltpu.VMEM((blk, 1), F32), pltpu.VMEM((blk, LANES), F32)],
        name="moba_attention",
        compiler_params=_params("parallel", "parallel", "arbitrary"),
    )(slopes, proj, proj, proj)


def _swa_kernel(slopes_ref, sinks_ref, q_ref, kp_ref, kc_ref, vp_ref, vc_ref, o_ref, *, group):
    w = SWA_WINDOW
    n = pl.program_id(1)
    g = pl.program_id(2)
    kcat = jnp.concatenate([kp_ref[...], kc_ref[...]], axis=0)
    vcat = jnp.concatenate([vp_ref[...], vc_ref[...]], axis=0)
    r = lax.broadcasted_iota(jnp.int32, (w, 2 * w), 0)
    c = lax.broadcasted_iota(jnp.int32, (w, 2 * w), 1)
    dist = r - c + w
    first_key = jnp.where(n > 0, 0, w)
    ok = jnp.logical_and(jnp.logical_and(dist >= 0, dist < w), c >= first_key)
    distf = dist.astype(F32)
    lane = lax.broadcasted_iota(jnp.int32, (w, LANES), 1)
    low = lane < SWA_HEAD_DIM
    for pair in range(group // 2):
        qp = q_ref[:, pair * LANES:(pair + 1) * LANES] * jnp.asarray(SWA_HEAD_DIM ** -0.5, BF16)
        outs = []
        for half in range(2):
            head = g * group + pair * 2 + half
            slope = slopes_ref[head]
            sink = sinks_ref[head]
            qz = jnp.where(low if half == 0 else jnp.logical_not(low), qp, jnp.zeros_like(qp))
            s = _nt_dot(qz, kcat) - slope * distf
            s = jnp.where(ok, s, NEG_INF)
            m = jnp.maximum(jnp.max(s, axis=1, keepdims=True), sink)
            e = jnp.exp(s - m)
            den = jnp.sum(e, axis=1, keepdims=True) + jnp.exp(sink - m)
            outs.append(jnp.dot(e.astype(BF16), vcat, preferred_element_type=F32) / den)
        o_ref[:, pair * LANES:(pair + 1) * LANES] = jnp.where(low, outs[0], outs[1]).astype(o_ref.dtype)


def _swa_attention(proj, k2, v2, sinks, slopes, batch, seq):
    w = SWA_WINDOW
    nblk = seq // w
    group = N_SWA_HEADS // N_SWA_KV_HEADS
    gw = group * SWA_HEAD_DIM
    prev = lambda b, n, g: (b * nblk + jnp.maximum(n - 1, 0), g)
    cur = lambda b, n, g: (b * nblk + n, g)
    return pl.pallas_call(
        functools.partial(_swa_kernel, group=group),
        grid=(batch, nblk, N_SWA_KV_HEADS),
        in_specs=[pl.BlockSpec(memory_space=pltpu.SMEM), pl.BlockSpec(memory_space=pltpu.SMEM),
                  pl.BlockSpec((w, gw), cur),
                  pl.BlockSpec((w, LANES), prev), pl.BlockSpec((w, LANES), cur),
                  pl.BlockSpec((w, LANES), prev), pl.BlockSpec((w, LANES), cur)],
        out_specs=pl.BlockSpec((w, gw), cur),
        out_shape=jax.ShapeDtypeStruct((batch * seq, N_SWA_HEADS * SWA_HEAD_DIM), BF16),
        name="swa_attention",
        compiler_params=_params("parallel", "parallel", "parallel"),
    )(slopes, sinks, proj, k2, k2, v2, v2)


def _layer_norm(z, g, b):
    mu = jnp.mean(z, axis=-1, keepdims=True)
    zc = z - mu
    var = jnp.mean(zc * zc, axis=-1, keepdims=True)
    return zc * lax.rsqrt(var + LN_EPS) * g + b


def _split3(a):
    hi = a.astype(BF16)
    r1 = a - hi.astype(F32)
    mid = r1.astype(BF16)
    return hi, mid, (r1 - mid.astype(F32)).astype(BF16)


def _norm_router_kernel(x_ref, h_ref, g_ref, b_ref, rw_ref, rb_ref,
                        y_ref, rank_ref, w_ref, pos_ref, cnt_ref, carry, *, tm):
    step = pl.program_id(0)

    @pl.when(step == 0)
    def _():
        carry[...] = jnp.zeros(carry.shape, F32)

    y = _layer_norm(DEEPNORM_ALPHA * x_ref[...] + h_ref[...], g_ref[...], b_ref[...])
    y_ref[...] = y

    yh, ym, yl = _split3(y)
    wh, wm, wl = _split3(rw_ref[...])
    dot = lambda a, b: jnp.dot(a, b, preferred_element_type=F32)
    logits = (dot(yl, wh) + dot(ym, wm) + dot(yh, wl)) + (dot(ym, wh) + dot(yh, wm)) + dot(yh, wh) + rb_ref[...]

    e_id = lax.broadcasted_iota(jnp.int32, logits.shape, 1)
    beaten = jnp.zeros(logits.shape, jnp.int32)
    for n in range(N_EXPERTS):
        ln = logits[:, n:n + 1]
        beaten = beaten + jnp.where(ln > logits, 1, jnp.where(ln == logits, jnp.where(e_id > n, 1, 0), 0))
    sel = beaten < TOP_K
    ex = jnp.where(sel, jnp.exp(logits - jnp.max(logits, axis=1, keepdims=True)), 0.0)
    rank_ref[...] = beaten
    w_ref[...] = ex / jnp.sum(ex, axis=1, keepdims=True)

    self = jnp.where(sel, 1.0, 0.0)
    rr = lax.broadcasted_iota(jnp.int32, (tm, tm), 0)
    cc = lax.broadcasted_iota(jnp.int32, (tm, tm), 1)
    before = jnp.where(rr > cc, 1.0, 0.0).astype(BF16)
    within = jnp.dot(before, self.astype(BF16), preferred_element_type=F32)
    pos_ref[...] = (carry[...] + within).astype(jnp.int32)
    carry[...] = carry[...] + jnp.sum(self, axis=0, keepdims=True)
    cnt_ref[...] = carry[...].astype(jnp.int32)


def _norm_router(x, h, g, b, rw, rb):
    t, d = x.shape
    tm = min(NORM_TM, t)
    ne = rw.shape[1]
    row = pl.BlockSpec((tm, d), lambda i: (i, 0))
    vec = pl.BlockSpec((1, d), lambda i: (0, 0))
    per_e = pl.BlockSpec((tm, ne), lambda i: (i, 0))
    return pl.pallas_call(
        functools.partial(_norm_router_kernel, tm=tm),
        grid=(t // tm,),
        in_specs=[row, row, vec, vec, pl.BlockSpec((d, ne), lambda i: (0, 0)), pl.BlockSpec((1, ne), lambda i: (0, 0))],
        out_specs=[row, per_e, per_e, per_e, pl.BlockSpec((1, ne), lambda i: (0, 0))],
        out_shape=[jax.ShapeDtypeStruct((t, d), F32), jax.ShapeDtypeStruct((t, ne), jnp.int32),
                   jax.ShapeDtypeStruct((t, ne), F32), jax.ShapeDtypeStruct((t, ne), jnp.int32),
                   jax.ShapeDtypeStruct((1, ne), jnp.int32)],
        scratch_shapes=[pltpu.VMEM((1, ne), F32)],
        name="norm_router",
        compiler_params=_params("arbitrary"),
    )(x, h, g.reshape(1, d), b.reshape(1, d), rw, rb.reshape(1, ne))


def _row_copy(src_hbm, dst_ref, src_row, dst_row, sem):
    return pltpu.make_async_copy(src_hbm.at[pl.ds(src_row, 1), :], dst_ref.at[pl.ds(dst_row, 1), :], sem)


def _gather_kernel(tok_ref, x_hbm, o_ref, sem, *, tg):
    base = pl.program_id(0) * tg

    def issue(r, carry):
        _row_copy(x_hbm, o_ref, tok_ref[base + r], r, sem).start()
        return carry

    lax.fori_loop(0, tg, issue, 0)

    def drain(r, carry):
        _row_copy(x_hbm, o_ref, 0, r, sem).wait()
        return carry

    lax.fori_loop(0, tg, drain, 0)


def _gather_rows(x, row_token):
    p = row_token.shape[0]
    d = x.shape[1]
    tg = min(GATHER_T, p)
    return pl.pallas_call(
        functools.partial(_gather_kernel, tg=tg),
        grid_spec=pltpu.PrefetchScalarGridSpec(
            num_scalar_prefetch=1,
            grid=(p // tg,),
            in_specs=[pl.BlockSpec(memory_space=pl.ANY)],
            out_specs=pl.BlockSpec((tg, d), lambda i, tok: (i, 0)),
            scratch_shapes=[pltpu.SemaphoreType.DMA(())]),
        out_shape=jax.ShapeDtypeStruct((p, d), x.dtype),
        name="dispatch_gather",
        compiler_params=_params("arbitrary"),
    )(row_token, x)


def _expert_changed(te_ref, t):
    return jnp.logical_or(t == 0, te_ref[t] != te_ref[jnp.maximum(t - 1, 0)])


def _gate_up_kernel(te_ref, nu_ref, x_ref, wg_ref, wu_ref, bg_ref, bu_ref, h_ref, wg_bf, wu_bf):
    t = pl.program_id(1)

    @pl.when(_expert_changed(te_ref, t))
    def _():
        wg_bf[...] = wg_ref[0].astype(BF16)
        wu_bf[...] = wu_ref[0].astype(BF16)

    @pl.when(t < nu_ref[0])
    def _():
        x = x_ref[...].astype(BF16)
        gate = jnp.minimum(jnp.dot(x, wg_bf[...], preferred_element_type=F32) + bg_ref[0], SWIGLU_LIMIT)
        up = jnp.clip(jnp.dot(x, wu_bf[...], preferred_element_type=F32) + bu_ref[0], -SWIGLU_LIMIT, SWIGLU_LIMIT)
        h_ref[...] = ((up + 1.0) * gate * jax.nn.sigmoid(SWIGLU_ALPHA * gate)).astype(h_ref.dtype)

    @pl.when(t >= nu_ref[0])
    def _():
        h_ref[...] = jnp.zeros(h_ref.shape, h_ref.dtype)


def _down_kernel(te_ref, nu_ref, h_ref, wd_ref, bd_ref, y_ref, wd_bf):
    t = pl.program_id(1)

    @pl.when(_expert_changed(te_ref, t))
    def _():
        wd_bf[...] = wd_ref[0].astype(BF16)

    @pl.when(t < nu_ref[0])
    def _():
        y_ref[...] = jnp.dot(h_ref[...], wd_bf[...], preferred_element_type=F32) + bd_ref[0]

    @pl.when(t >= nu_ref[0])
    def _():
        y_ref[...] = jnp.zeros(y_ref.shape, y_ref.dtype)


def _expert_ffn(xs, tile_expert, n_used, w_gate, b_gate, w_up, b_up, w_down, b_down):
    p, d = xs.shape
    ne, _, f = w_gate.shape
    tm = MOE_TM
    tf = min(MOE_TF, f)
    tn = min(MOE_TN, d)
    nt = p // tm
    used = lambda t, nu: jnp.minimum(t, nu[0] - 1)
    h = pl.pallas_call(
        _gate_up_kernel,
        grid_spec=pltpu.PrefetchScalarGridSpec(
            num_scalar_prefetch=2,
            grid=(f // tf, nt),
            in_specs=[pl.BlockSpec((tm, d), lambda c, t, te, nu: (used(t, nu), 0)),
                      pl.BlockSpec((1, d, tf), lambda c, t, te, nu: (te[t], 0, c)),
                      pl.BlockSpec((1, d, tf), lambda c, t, te, nu: (te[t], 0, c)),
                      pl.BlockSpec((1, 1, tf), lambda c, t, te, nu: (te[t], 0, c)),
                      pl.BlockSpec((1, 1, tf), lambda c, t, te, nu: (te[t], 0, c))],
            out_specs=pl.BlockSpec((tm, tf), lambda c, t, te, nu: (t, c)),
            scratch_shapes=[pltpu.VMEM((d, tf), BF16), pltpu.VMEM((d, tf), BF16)]),
        out_shape=jax.ShapeDtypeStruct((p, f), BF16),
        name="expert_gate_up",
        compiler_params=_params("arbitrary", "arbitrary"),
    )(tile_expert, n_used, xs, w_gate, w_up, b_gate.reshape(ne, 1, f), b_up.reshape(ne, 1, f))
    return pl.pallas_call(
        _down_kernel,
        grid_spec=pltpu.PrefetchScalarGridSpec(
            num_scalar_prefetch=2,
            grid=(d // tn, nt),
            in_specs=[pl.BlockSpec((tm, f), lambda c, t, te, nu: (used(t, nu), 0)),
                      pl.BlockSpec((1, f, tn), lambda c, t, te, nu: (te[t], 0, c)),
                      pl.BlockSpec((1, 1, tn), lambda c, t, te, nu: (te[t], 0, c))],
            out_specs=pl.BlockSpec((tm, tn), lambda c, t, te, nu: (t, c)),
            scratch_shapes=[pltpu.VMEM((f, tn), BF16)]),
        out_shape=jax.ShapeDtypeStruct((p, d), F32),
        name="expert_down",
        compiler_params=_params("arbitrary", "arbitrary"),
    )(tile_expert, n_used, h, w_down, b_down.reshape(ne, 1, d))


def _combine_norm_kernel(slot_ref, x_ref, w_ref, g_ref, b_ref, ys_hbm, o_ref, buf, sem, *, tc):
    base = pl.program_id(0) * tc * TOP_K

    def issue(r, carry):
        for k in range(TOP_K):
            _row_copy(ys_hbm, buf.at[k], slot_ref[base + r * TOP_K + k], r, sem).start()
        return carry

    lax.fori_loop(0, tc, issue, 0)

    def drain(r, carry):
        for k in range(TOP_K):
            _row_copy(ys_hbm, buf.at[k], 0, r, sem).wait()
        return carry

    lax.fori_loop(0, tc, drain, 0)

    w = w_ref[...]
    moe = w[:, 0:1] * buf[0]
    for k in range(1, TOP_K):
        moe = moe + w[:, k:k + 1] * buf[k]
    o_ref[...] = _layer_norm(DEEPNORM_ALPHA * x_ref[...] + moe, g_ref[...], b_ref[...])


def _combine_norm(x, ys, slots, w4, g, b):
    t, d = x.shape
    tc = min(COMBINE_T, t)
    return pl.pallas_call(
        functools.partial(_combine_norm_kernel, tc=tc),
        grid_spec=pltpu.PrefetchScalarGridSpec(
            num_scalar_prefetch=1,
            grid=(t // tc,),
            in_specs=[pl.BlockSpec((tc, d), lambda i, s: (i, 0)),
                      pl.BlockSpec((tc, TOP_K), lambda i, s: (i, 0)),
                      pl.BlockSpec((1, d), lambda i, s: (0, 0)),
                      pl.BlockSpec((1, d), lambda i, s: (0, 0)),
                      pl.BlockSpec(memory_space=pl.ANY)],
            out_specs=pl.BlockSpec((tc, d), lambda i, s: (i, 0)),
            scratch_shapes=[pltpu.VMEM((TOP_K, tc, d), F32), pltpu.SemaphoreType.DMA(())]),
        out_shape=jax.ShapeDtypeStruct((t, d), F32),
        name="combine_norm",
        compiler_params=_params("arbitrary"),
    )(slots, x, w4, g.reshape(1, d), b.reshape(1, d), ys)


def _routing_tables(rank, weight, pos, counts, tm):
    t, ne = rank.shape
    counts = counts[0]
    padded = ((counts + tm - 1) // tm) * tm
    ends = jnp.cumsum(padded)
    starts = ends - padded
    n_tiles = (t * TOP_K) // tm + ne
    n_used = (ends[-1] // tm).astype(jnp.int32)
    tile_start = jnp.arange(n_tiles, dtype=jnp.int32) * tm
    tile_expert = jnp.minimum(jnp.searchsorted(ends, tile_start, side="right"), ne - 1).astype(jnp.int32)
    last = tile_expert[jnp.maximum(n_used - 1, 0)]
    tile_expert = jnp.where(jnp.arange(n_tiles) < n_used, tile_expert, last)
    slot_dense = starts[None, :].astype(jnp.int32) + pos
    pick = lambda a, k: jnp.sum(jnp.where(rank == k, a, jnp.zeros_like(a)), axis=1)
    slots = jnp.stack([pick(slot_dense, k) for k in range(TOP_K)], axis=1)
    w4 = jnp.stack([pick(weight, k) for k in range(TOP_K)], axis=1)
    token = jnp.repeat(jnp.arange(t, dtype=jnp.int32), TOP_K)
    row_token = jnp.zeros((n_tiles * tm,), jnp.int32).at[slots.reshape(-1)].set(token)
    return row_token, tile_expert, n_used.reshape(1), slots.reshape(-1), w4


def _post_norm_moe(x, h, ln1_g, ln1_b, router_w, router_b, w_gate, b_gate, w_up, b_up, w_down, b_down,
                   ln2_g, ln2_b):
    y, rank, weight, pos, counts = _norm_router(x, h, ln1_g, ln1_b, router_w, router_b)
    row_token, tile_expert, n_used, slots, w4 = _routing_tables(rank, weight, pos, counts, MOE_TM)
    xs = _gather_rows(y, row_token)
    ys = _expert_ffn(xs, tile_expert, n_used, w_gate, b_gate, w_up, b_up, w_down, b_down)
    return _combine_norm(y, ys, slots, w4, ln2_g, ln2_b)


def kernel(x, l0_w_in, l0_lambda_q1, l0_lambda_k1, l0_lambda_q2, l0_lambda_k2, l0_subln_g, l0_w_o, l0_ln1_g, l0_ln1_b, l0_router_w, l0_router_b, l0_w_gate, l0_b_gate, l0_w_up, l0_b_up, l0_w_down, l0_b_down, l0_ln2_g, l0_ln2_b, l1_w_in, l1_sinks, l1_w_o, l1_ln1_g, l1_ln1_b, l1_router_w, l1_router_b, l1_w_gate, l1_b_gate, l1_w_up, l1_b_up, l1_w_down, l1_b_down, l1_ln2_g, l1_ln2_b):
    batch, seq, d = x.shape
    xt = x.reshape(batch * seq, d)

    proj = _matmul(xt.astype(BF16), l0_w_in.astype(BF16), BF16)
    slopes = _alibi_slopes(N_DIFF_HEADS + N_MOBA_HEADS)
    lam_init = 0.8 - 0.6 * math.exp(-0.3 * 0)
    attn_a = _diff_attention(proj, l0_lambda_q1, l0_lambda_k1, l0_lambda_q2, l0_lambda_k2, l0_subln_g,
                             slopes[0::2], batch, seq, lam_init)
    attn_b = _moba_attention(proj, slopes[1::2], batch, seq)
    h = _matmul2(attn_a, attn_b, l0_w_o.astype(BF16), F32)
    xt = _post_norm_moe(xt, h, l0_ln1_g, l0_ln1_b, l0_router_w, l0_router_b, l0_w_gate, l0_b_gate,
                        l0_w_up, l0_b_up, l0_w_down, l0_b_down, l0_ln2_g, l0_ln2_b)

    proj = _matmul(xt.astype(BF16), l1_w_in.astype(BF16), BF16)
    q_w = N_SWA_HEADS * SWA_HEAD_DIM
    kv_w = N_SWA_KV_HEADS * SWA_HEAD_DIM
    dup = lambda a: jnp.broadcast_to(a.reshape(-1, N_SWA_KV_HEADS, 1, SWA_HEAD_DIM),
                                     (a.shape[0], N_SWA_KV_HEADS, 2, SWA_HEAD_DIM)).reshape(a.shape[0], 2 * kv_w)
    k2 = dup(proj[:, q_w:q_w + kv_w])
    v2 = dup(proj[:, q_w + kv_w:])
    attn = _swa_attention(proj, k2, v2, l1_sinks, _alibi_slopes(N_SWA_HEADS), batch, seq)
    h = _matmul(attn, l1_w_o.astype(BF16), F32)
    xt = _post_norm_moe(xt, h, l1_ln1_g, l1_ln1_b, l1_router_w, l1_router_b, l1_w_gate, l1_b_gate,
                        l1_w_up, l1_b_up, l1_w_down, l1_b_down, l1_ln2_g, l1_ln2_b)
    return xt.reshape(batch, seq, d)
```

```python
import functools
import math

import jax
import jax.numpy as jnp
from jax import lax
from jax.experimental import pallas as pl
from jax.experimental.pallas import tpu as pltpu

F32 = jnp.float32
BF16 = jnp.bfloat16
NEG_INF = float("-inf")

DEPTH = 2
DIFF_QK_DIM = 64
N_DIFF_HEADS = 16
N_MOBA_HEADS = 16
MOBA_BLOCK = 256
MOBA_TOPK = 3
N_SWA_HEADS = 64
N_SWA_KV_HEADS = 8
SWA_HEAD_DIM = 64
SWA_WINDOW = 128
N_EXPERTS = 32
TOP_K = 4
SWIGLU_LIMIT = 7.0
SWIGLU_ALPHA = 1.702
LN_EPS = 1e-5
SUBLN_EPS = 1e-5
DEEPNORM_ALPHA = (2.0 * DEPTH) ** 0.25

LANES = 128
VMEM_LIMIT = 56 * 1024 * 1024

MM_TM, MM_TN = 512, 1024
DIFF_T = 512
MOBA_T = 512
SWA_T = SWA_WINDOW
NORM_TM = 256
MOE_TM = 256
MOE_TF = 512
MOE_TN = 2048
DISPATCH_T = 128
COMBINE_T = 128


def _params(*sem):
    return pltpu.CompilerParams(dimension_semantics=sem, vmem_limit_bytes=VMEM_LIMIT)


def _alibi_slopes(n):
    return 2.0 ** (-8.0 * (jnp.arange(n, dtype=F32) + 1.0) / n)


def _nt_dot(a, b):
    return lax.dot_general(a, b, (((1,), (1,)), ((), ())), preferred_element_type=F32)


def _matmul_kernel(x_ref, w_ref, o_ref):
    o_ref[...] = jnp.dot(x_ref[...], w_ref[...], preferred_element_type=F32).astype(o_ref.dtype)


def _matmul(x, w, out_dtype, tm=MM_TM, tn=MM_TN):
    m, k = x.shape
    n = w.shape[1]
    tm, tn = min(tm, m), min(tn, n)
    return pl.pallas_call(
        _matmul_kernel,
        grid=(n // tn, m // tm),
        in_specs=[pl.BlockSpec((tm, k), lambda j, i: (i, 0)),
                  pl.BlockSpec((k, tn), lambda j, i: (0, j))],
        out_specs=pl.BlockSpec((tm, tn), lambda j, i: (i, j)),
        out_shape=jax.ShapeDtypeStruct((m, n), out_dtype),
        name="matmul",
        compiler_params=_params("parallel", "parallel"),
    )(x, w)


def _matmul2_kernel(xa_ref, xb_ref, w_ref, o_ref, *, ka):
    acc = jnp.dot(xa_ref[...], w_ref[:ka, :], preferred_element_type=F32)
    acc = acc + jnp.dot(xb_ref[...], w_ref[ka:, :], preferred_element_type=F32)
    o_ref[...] = acc.astype(o_ref.dtype)


def _matmul2(xa, xb, w, out_dtype, tm=MM_TM, tn=MM_TN):
    m, ka = xa.shape
    kb = xb.shape[1]
    n = w.shape[1]
    tm, tn = min(tm, m), min(tn, n)
    return pl.pallas_call(
        functools.partial(_matmul2_kernel, ka=ka),
        grid=(n // tn, m // tm),
        in_specs=[pl.BlockSpec((tm, ka), lambda j, i: (i, 0)),
                  pl.BlockSpec((tm, kb), lambda j, i: (i, 0)),
                  pl.BlockSpec((ka + kb, tn), lambda j, i: (0, j))],
        out_specs=pl.BlockSpec((tm, tn), lambda j, i: (i, j)),
        out_shape=jax.ShapeDtypeStruct((m, n), out_dtype),
        name="matmul2",
        compiler_params=_params("parallel", "parallel"),
    )(xa, xb, w)


def _softmax_step(m_ref, l_ref, a_ref, st, shift, vt):
    m_prev = m_ref[...]
    m_next = jnp.maximum(m_prev, jnp.max(st, axis=0, keepdims=True) - shift)
    alpha = jnp.exp(m_prev - m_next)
    p = jnp.exp(st - (m_next + shift))
    l_ref[...] = alpha * l_ref[...] + jnp.sum(p, axis=0, keepdims=True)
    a_ref[...] = alpha * a_ref[...] + jnp.dot(vt, p.astype(BF16), preferred_element_type=F32)
    m_ref[...] = m_next


def _softmax_init(m_ref, l_ref, a_ref):
    m_ref[...] = jnp.full(m_ref.shape, NEG_INF, F32)
    l_ref[...] = jnp.zeros(l_ref.shape, F32)
    a_ref[...] = jnp.zeros(a_ref.shape, F32)


def _diff_attn_kernel(slopes_ref, lq1_ref, lk1_ref, lq2_ref, lk2_ref, g_ref, q_ref, k_ref, vt_ref, o_ref,
                      m1, l1, a1, m2, l2, a2, *, t, lam_init):
    h = pl.program_id(1)
    i = pl.program_id(2)
    slope = slopes_ref[h]
    lam = (jnp.exp(jnp.sum(lq1_ref[...] * lk1_ref[...], keepdims=True))
           - jnp.exp(jnp.sum(lq2_ref[...] * lk2_ref[...], keepdims=True)) + lam_init)

    q = q_ref[...] * jnp.asarray(DIFF_QK_DIM ** -0.5, BF16)
    lane = lax.broadcasted_iota(jnp.int32, q.shape, 1)
    q1 = jnp.where(lane < DIFF_QK_DIM, q, jnp.zeros_like(q))
    q2 = jnp.where(lane >= DIFF_QK_DIM, q, jnp.zeros_like(q))

    rel = lax.broadcasted_iota(jnp.int32, (t, t), 1) - lax.broadcasted_iota(jnp.int32, (t, t), 0)
    bias0 = -slope * rel.astype(F32)

    _softmax_init(m1, l1, a1)
    _softmax_init(m2, l2, a2)

    def tile(j, shift, causal):
        kt = k_ref[pl.ds(pl.multiple_of(j * t, t), t), :]
        vt = vt_ref[j]
        for qz, (m_r, l_r, a_r) in ((q1, (m1, l1, a1)), (q2, (m2, l2, a2))):
            st = _nt_dot(kt, qz) + bias0
            if causal:
                st = jnp.where(rel >= 0, st, NEG_INF)
            _softmax_step(m_r, l_r, a_r, st, shift, vt)

    tile(i, 0.0, True)

    def body(j, carry):
        tile(j, slope * ((i - j) * t).astype(F32), False)
        return carry

    lax.fori_loop(0, i, body, 0)

    ot = a1[...] / l1[...] - lam * (a2[...] / l2[...])
    ot = ot * lax.rsqrt(jnp.mean(ot * ot, axis=0, keepdims=True) + SUBLN_EPS)
    o_ref[...] = (ot.T * g_ref[...] * (1.0 - lam_init)).astype(o_ref.dtype)


def _tile_transposed_values(v, batch, seq, nh, t):
    v = v.reshape(batch, seq // t, t, nh, LANES).transpose(0, 3, 1, 4, 2)
    return v.reshape(batch * nh, seq // t, LANES, t)


def _diff_attention(proj, lq1, lk1, lq2, lk2, subln_g, slopes, batch, seq, lam_init):
    t = min(DIFF_T, seq)
    nq = seq // t
    nh = N_DIFF_HEADS
    vt = _tile_transposed_values(proj[:, 2 * nh * LANES:3 * nh * LANES], batch, seq, nh, t)
    vec = lambda n: pl.BlockSpec((1, n), lambda b, h, i: (0, 0))
    kern = functools.partial(_diff_attn_kernel, t=t, lam_init=lam_init)
    stat = pltpu.VMEM((1, t), F32)
    acc = pltpu.VMEM((LANES, t), F32)
    return pl.pallas_call(
        kern,
        grid=(batch, nh, nq),
        in_specs=[pl.BlockSpec(memory_space=pltpu.SMEM),
                  vec(DIFF_QK_DIM), vec(DIFF_QK_DIM), vec(DIFF_QK_DIM), vec(DIFF_QK_DIM), vec(LANES),
                  pl.BlockSpec((t, LANES), lambda b, h, i: (b * nq + i, h)),
                  pl.BlockSpec((seq, LANES), lambda b, h, i: (b, nh + h)),
                  pl.BlockSpec((None, nq, LANES, t), lambda b, h, i: (b * nh + h, 0, 0, 0))],
        out_specs=pl.BlockSpec((t, LANES), lambda b, h, i: (b * nq + i, h)),
        out_shape=jax.ShapeDtypeStruct((batch * seq, nh * LANES), BF16),
        scratch_shapes=[stat, stat, acc, stat, stat, acc],
        name="diff_attention",
        compiler_params=_params("parallel", "parallel", "parallel"),
    )(slopes, lq1.reshape(1, -1), lk1.reshape(1, -1), lq2.reshape(1, -1), lk2.reshape(1, -1),
      subln_g.reshape(1, -1), proj, proj, vt)


def _moba_kernel(slopes_ref, q_ref, k_ref, vt_ref, o_ref, km_hi, km_mid, km_lo, selb, m_r, l_r, a_r, *, nb, t):
    blk = MOBA_BLOCK
    per = t // blk
    h = pl.program_id(1)
    i = pl.program_id(2)
    slope = slopes_ref[h]

    @pl.when(i == 0)
    def _():
        km = jnp.sum(k_ref[...].astype(F32).reshape(nb, blk, LANES), axis=1) * (1.0 / blk)
        hi = km.astype(BF16)
        r1 = km - hi.astype(F32)
        mid = r1.astype(BF16)
        km_hi[...] = hi
        km_mid[...] = mid
        km_lo[...] = (r1 - mid.astype(F32)).astype(BF16)

    q = q_ref[...]
    gate = _nt_dot(km_hi[...], q) + _nt_dot(km_mid[...], q) + _nt_dot(km_lo[...], q)
    blk_id = lax.broadcasted_iota(jnp.int32, gate.shape, 0)
    q_sub = lax.broadcasted_iota(jnp.int32, (1, t), 1) // blk
    past = blk_id < i * per + q_sub
    gate = jnp.where(past, gate, NEG_INF)
    beaten = jnp.zeros(gate.shape, jnp.int32)
    for n in range(nb):
        gn = gate[n:n + 1, :]
        beaten = beaten + jnp.where(gn > gate, 1, jnp.where(gn == gate, jnp.where(blk_id > n, 1, 0), 0))
    selb[...] = jnp.where(past, jnp.where(beaten < MOBA_TOPK, 0.0, NEG_INF), NEG_INF)

    rel = lax.broadcasted_iota(jnp.int32, (t, t), 1) - lax.broadcasted_iota(jnp.int32, (t, t), 0)
    bias0 = -slope * rel.astype(F32)
    scale = LANES ** -0.5
    _softmax_init(m_r, l_r, a_r)

    def block_rows(j, own):
        rows = []
        for u in range(per):
            row = selb[pl.ds(j * per + u, 1), :]
            if own:
                row = jnp.where(q_sub <= u, 0.0, row)
            rows.append(jnp.broadcast_to(row, (blk, t)))
        return jnp.concatenate(rows, axis=0)

    def scores(j):
        return _nt_dot(k_ref[pl.ds(pl.multiple_of(j * t, t), t), :], q) * scale + bias0

    st = jnp.where(rel >= 0, scores(i) + block_rows(i, True), NEG_INF)
    _softmax_step(m_r, l_r, a_r, st, 0.0, vt_ref[i])

    def body(j, carry):
        _softmax_step(m_r, l_r, a_r, scores(j) + block_rows(j, False), slope * ((i - j) * t).astype(F32), vt_ref[j])
        return carry

    lax.fori_loop(0, i, body, 0)
    o_ref[...] = (a_r[...] / l_r[...]).T.astype(o_ref.dtype)


def _moba_attention(proj, slopes, batch, seq):
    blk = MOBA_BLOCK
    t = min(MOBA_T, seq)
    nq = seq // t
    nb = seq // blk
    nh = N_MOBA_HEADS
    base = 3 * N_DIFF_HEADS
    vt = _tile_transposed_values(proj[:, (base + 2 * nh) * LANES:(base + 3 * nh) * LANES], batch, seq, nh, t)
    kern = functools.partial(_moba_kernel, nb=nb, t=t)
    return pl.pallas_call(
        kern,
        grid=(batch, nh, nq),
        in_specs=[pl.BlockSpec(memory_space=pltpu.SMEM),
                  pl.BlockSpec((t, LANES), lambda b, h, i: (b * nq + i, base + h)),
                  pl.BlockSpec((seq, LANES), lambda b, h, i: (b, base + nh + h)),
                  pl.BlockSpec((None, nq, LANES, t), lambda b, h, i: (b * nh + h, 0, 0, 0))],
        out_specs=pl.BlockSpec((t, LANES), lambda b, h, i: (b * nq + i, h)),
        out_shape=jax.ShapeDtypeStruct((batch * seq, nh * LANES), BF16),
        scratch_shapes=[pltpu.VMEM((nb, LANES), BF16), pltpu.VMEM((nb, LANES), BF16), pltpu.VMEM((nb, LANES), BF16),
                        pltpu.VMEM((nb, t), F32),
                        pltpu.VMEM((1, t), F32), pltpu.VMEM((1, t), F32), pltpu.VMEM((LANES, t), F32)],
        name="moba_attention",
        compiler_params=_params("parallel", "parallel", "arbitrary"),
    )(slopes, proj, proj, vt)


def _swa_kernel(slopes_ref, sinks_ref, q_ref, kp_ref, kc_ref, vp_ref, vc_ref, o_ref, *, group):
    w = SWA_WINDOW
    n = pl.program_id(1)
    g = pl.program_id(2)
    kcat = jnp.concatenate([kp_ref[...], kc_ref[...]], axis=0)
    vcat = jnp.concatenate([vp_ref[...], vc_ref[...]], axis=0)
    r = lax.broadcasted_iota(jnp.int32, (w, 2 * w), 0)
    c = lax.broadcasted_iota(jnp.int32, (w, 2 * w), 1)
    dist = r - c + w
    first_key = jnp.where(n > 0, 0, w)
    ok = jnp.logical_and(jnp.logical_and(dist >= 0, dist < w), c >= first_key)
    distf = dist.astype(F32)
    lane = lax.broadcasted_iota(jnp.int32, (w, LANES), 1)
    low = lane < SWA_HEAD_DIM
    for pair in range(group // 2):
        qp = q_ref[:, pair * LANES:(pair + 1) * LANES] * jnp.asarray(SWA_HEAD_DIM ** -0.5, BF16)
        outs = []
        for half in range(2):
            head = g * group + pair * 2 + half
            slope = slopes_ref[head]
            sink = sinks_ref[head]
            qz = jnp.where(low if half == 0 else jnp.logical_not(low), qp, jnp.zeros_like(qp))
            s = _nt_dot(qz, kcat) - slope * distf
            s = jnp.where(ok, s, NEG_INF)
            m = jnp.maximum(jnp.max(s, axis=1, keepdims=True), sink)
            e = jnp.exp(s - m)
            den = jnp.sum(e, axis=1, keepdims=True) + jnp.exp(sink - m)
            outs.append(jnp.dot(e.astype(BF16), vcat, preferred_element_type=F32) / den)
        o_ref[:, pair * LANES:(pair + 1) * LANES] = jnp.where(low, outs[0], outs[1]).astype(o_ref.dtype)


def _swa_attention(proj, k2, v2, sinks, slopes, batch, seq):
    w = SWA_WINDOW
    nblk = seq // w
    group = N_SWA_HEADS // N_SWA_KV_HEADS
    gw = group * SWA_HEAD_DIM
    prev = lambda b, n, g: (b * nblk + jnp.maximum(n - 1, 0), g)
    cur = lambda b, n, g: (b * nblk + n, g)
    return pl.pallas_call(
        functools.partial(_swa_kernel, group=group),
        grid=(batch, nblk, N_SWA_KV_HEADS),
        in_specs=[pl.BlockSpec(memory_space=pltpu.SMEM), pl.BlockSpec(memory_space=pltpu.SMEM),
                  pl.BlockSpec((w, gw), cur),
                  pl.BlockSpec((w, LANES), prev), pl.BlockSpec((w, LANES), cur),
                  pl.BlockSpec((w, LANES), prev), pl.BlockSpec((w, LANES), cur)],
        out_specs=pl.BlockSpec((w, gw), cur),
        out_shape=jax.ShapeDtypeStruct((batch * seq, N_SWA_HEADS * SWA_HEAD_DIM), BF16),
        name="swa_attention",
        compiler_params=_params("parallel", "parallel", "parallel"),
    )(slopes, sinks, proj, k2, k2, v2, v2)


def _layer_norm(z, g, b):
    mu = jnp.mean(z, axis=-1, keepdims=True)
    zc = z - mu
    var = jnp.mean(zc * zc, axis=-1, keepdims=True)
    return zc * lax.rsqrt(var + LN_EPS) * g + b


def _split3(a):
    hi = a.astype(BF16)
    r1 = a - hi.astype(F32)
    mid = r1.astype(BF16)
    return hi, mid, (r1 - mid.astype(F32)).astype(BF16)


def _norm_router_kernel(x_ref, h_ref, g_ref, b_ref, rw_ref, rb_ref,
                        y_ref, eid_ref, w_ref, pos_ref, cnt_ref, carry, *, tm):
    step = pl.program_id(0)

    @pl.when(step == 0)
    def _():
        carry[...] = jnp.zeros(carry.shape, F32)

    y = _layer_norm(DEEPNORM_ALPHA * x_ref[...] + h_ref[...], g_ref[...], b_ref[...])
    y_ref[...] = y

    yh, ym, yl = _split3(y)
    wh, wm, wl = _split3(rw_ref[...])
    dot = lambda a, b: jnp.dot(a, b, preferred_element_type=F32)
    logits = (dot(yl, wh) + dot(ym, wm) + dot(yh, wl)) + (dot(ym, wh) + dot(yh, wm)) + dot(yh, wh) + rb_ref[...]

    e_id = lax.broadcasted_iota(jnp.int32, logits.shape, 1)
    beaten = jnp.zeros(logits.shape, jnp.int32)
    for n in range(N_EXPERTS):
        ln = logits[:, n:n + 1]
        beaten = beaten + jnp.where(ln > logits, 1, jnp.where(ln == logits, jnp.where(e_id > n, 1, 0), 0))
    sel = beaten < TOP_K
    ex = jnp.where(sel, jnp.exp(logits - jnp.max(logits, axis=1, keepdims=True)), 0.0)
    weight = ex / jnp.sum(ex, axis=1, keepdims=True)

    self = jnp.where(sel, 1.0, 0.0)
    rr = lax.broadcasted_iota(jnp.int32, (tm, tm), 0)
    cc = lax.broadcasted_iota(jnp.int32, (tm, tm), 1)
    before = jnp.where(rr > cc, 1.0, 0.0).astype(BF16)
    pos = carry[...] + jnp.dot(before, self.astype(BF16), preferred_element_type=F32)
    carry[...] = carry[...] + jnp.sum(self, axis=0, keepdims=True)
    cnt_ref[...] = carry[...].astype(jnp.int32)

    e_f = e_id.astype(F32)
    k_id = lax.broadcasted_iota(jnp.int32, (tm, TOP_K), 1)
    e_k = jnp.zeros((tm, TOP_K), F32)
    w_k = jnp.zeros((tm, TOP_K), F32)
    p_k = jnp.zeros((tm, TOP_K), F32)
    for k in range(TOP_K):
        hit = beaten == k
        pick = lambda a: jnp.sum(jnp.where(hit, a, 0.0), axis=1, keepdims=True)
        e_k = jnp.where(k_id == k, pick(e_f), e_k)
        w_k = jnp.where(k_id == k, pick(weight), w_k)
        p_k = jnp.where(k_id == k, pick(pos), p_k)
    eid_ref[...] = e_k.astype(jnp.int32)
    w_ref[...] = w_k
    pos_ref[...] = p_k.astype(jnp.int32)


def _norm_router(x, h, g, b, rw, rb):
    t, d = x.shape
    tm = min(NORM_TM, t)
    ne = rw.shape[1]
    row = pl.BlockSpec((tm, d), lambda i: (i, 0))
    vec = pl.BlockSpec((1, d), lambda i: (0, 0))
    per_k = pl.BlockSpec((tm, TOP_K), lambda i: (i, 0))
    return pl.pallas_call(
        functools.partial(_norm_router_kernel, tm=tm),
        grid=(t // tm,),
        in_specs=[row, row, vec, vec, pl.BlockSpec((d, ne), lambda i: (0, 0)), pl.BlockSpec((1, ne), lambda i: (0, 0))],
        out_specs=[row, per_k, per_k, per_k, pl.BlockSpec((1, ne), lambda i: (0, 0))],
        out_shape=[jax.ShapeDtypeStruct((t, d), F32), jax.ShapeDtypeStruct((t, TOP_K), jnp.int32),
                   jax.ShapeDtypeStruct((t, TOP_K), F32), jax.ShapeDtypeStruct((t, TOP_K), jnp.int32),
                   jax.ShapeDtypeStruct((1, ne), jnp.int32)],
        scratch_shapes=[pltpu.VMEM((1, ne), F32)],
        name="norm_router",
        compiler_params=_params("arbitrary"),
    )(x, h, g.reshape(1, d), b.reshape(1, d), rw, rb.reshape(1, ne))


def _row_copy(src_ref, dst_ref, src_row, dst_row, sem):
    return pltpu.make_async_copy(src_ref.at[pl.ds(src_row, 1), :], dst_ref.at[pl.ds(dst_row, 1), :], sem)


def _dispatch_kernel(slot_ref, pad_lo_ref, pad_hi_ref, nu_ref, y_ref, xs_hbm, zrow, sem, pad_sem, *, tm, ne):
    step = pl.program_id(0)

    @pl.when(step == 0)
    def _():
        zrow[...] = jnp.zeros(zrow.shape, zrow.dtype)
        tile_rows = zrow.shape[0]
        tail = lambda tl: pltpu.make_async_copy(
            zrow, xs_hbm.at[pl.ds(pl.multiple_of(tl * tile_rows, tile_rows), tile_rows), :], pad_sem)

        def fill_tile(tl, carry):
            tail(tl).start()
            return carry

        def tile_filled(tl, carry):
            tail(tl).wait()
            return carry

        lax.fori_loop(nu_ref[0], xs_hbm.shape[0] // tile_rows, fill_tile, 0)
        lax.fori_loop(nu_ref[0], xs_hbm.shape[0] // tile_rows, tile_filled, 0)
        for e in range(ne):
            lo = pad_lo_ref[e]
            hi = pad_hi_ref[e]

            def fill(r, carry):
                _row_copy(zrow, xs_hbm, 0, r, pad_sem).start()
                return carry

            def filled(r, carry):
                _row_copy(zrow, xs_hbm, 0, r, pad_sem).wait()
                return carry

            lax.fori_loop(lo, hi, fill, 0)
            lax.fori_loop(lo, hi, filled, 0)

    base = step * tm * TOP_K

    def issue(r, carry):
        for k in range(TOP_K):
            _row_copy(y_ref, xs_hbm, r, slot_ref[base + r * TOP_K + k], sem).start()
        return carry

    def drain(r, carry):
        for k in range(TOP_K):
            _row_copy(y_ref, xs_hbm, r, slot_ref[base + r * TOP_K + k], sem).wait()
        return carry

    lax.fori_loop(0, tm, issue, 0, unroll=4)
    lax.fori_loop(0, tm, drain, 0, unroll=4)


def _dispatch_rows(y, slots, pad_lo, pad_hi, n_used, n_rows):
    t, d = y.shape
    tm = min(DISPATCH_T, t)
    ne = pad_lo.shape[0]
    return pl.pallas_call(
        functools.partial(_dispatch_kernel, tm=tm, ne=ne),
        grid_spec=pltpu.PrefetchScalarGridSpec(
            num_scalar_prefetch=4,
            grid=(t // tm,),
            in_specs=[pl.BlockSpec((tm, d), lambda i, s, lo, hi, nu: (i, 0))],
            out_specs=pl.BlockSpec(memory_space=pl.ANY),
            scratch_shapes=[pltpu.VMEM((MOE_TM, d), y.dtype), pltpu.SemaphoreType.DMA(()),
                            pltpu.SemaphoreType.DMA(())]),
        out_shape=jax.ShapeDtypeStruct((n_rows, d), y.dtype),
        name="dispatch_scatter",
        compiler_params=_params("arbitrary"),
    )(slots, pad_lo, pad_hi, n_used, y)


def _expert_changed(te_ref, t):
    return jnp.logical_or(t == 0, te_ref[t] != te_ref[jnp.maximum(t - 1, 0)])


def _gate_up_kernel(te_ref, nu_ref, x_ref, wg_ref, wu_ref, bg_ref, bu_ref, h_ref, wg_bf, wu_bf):
    t = pl.program_id(1)

    @pl.when(_expert_changed(te_ref, t))
    def _():
        wg_bf[...] = wg_ref[0].astype(BF16)
        wu_bf[...] = wu_ref[0].astype(BF16)

    @pl.when(t < nu_ref[0])
    def _():
        x = x_ref[...].astype(BF16)
        gate = jnp.minimum(jnp.dot(x, wg_bf[...], preferred_element_type=F32) + bg_ref[0], SWIGLU_LIMIT)
        up = jnp.clip(jnp.dot(x, wu_bf[...], preferred_element_type=F32) + bu_ref[0], -SWIGLU_LIMIT, SWIGLU_LIMIT)
        h_ref[...] = ((up + 1.0) * gate * jax.nn.sigmoid(SWIGLU_ALPHA * gate)).astype(h_ref.dtype)

    @pl.when(t >= nu_ref[0])
    def _():
        h_ref[...] = jnp.zeros(h_ref.shape, h_ref.dtype)


def _down_kernel(te_ref, nu_ref, h_ref, wd_ref, bd_ref, y_ref, wd_bf):
    t = pl.program_id(1)

    @pl.when(_expert_changed(te_ref, t))
    def _():
        wd_bf[...] = wd_ref[0].astype(BF16)

    @pl.when(t < nu_ref[0])
    def _():
        y_ref[...] = jnp.dot(h_ref[...], wd_bf[...], preferred_element_type=F32) + bd_ref[0]

    @pl.when(t >= nu_ref[0])
    def _():
        y_ref[...] = jnp.zeros(y_ref.shape, y_ref.dtype)


def _expert_ffn(xs, tile_expert, n_used, w_gate, b_gate, w_up, b_up, w_down, b_down):
    p, d = xs.shape
    ne, _, f = w_gate.shape
    tm = MOE_TM
    tf = min(MOE_TF, f)
    tn = min(MOE_TN, d)
    nt = p // tm
    used = lambda t, nu: jnp.minimum(t, nu[0] - 1)
    h = pl.pallas_call(
        _gate_up_kernel,
        grid_spec=pltpu.PrefetchScalarGridSpec(
            num_scalar_prefetch=2,
            grid=(f // tf, nt),
            in_specs=[pl.BlockSpec((tm, d), lambda c, t, te, nu: (used(t, nu), 0)),
                      pl.BlockSpec((1, d, tf), lambda c, t, te, nu: (te[t], 0, c)),
                      pl.BlockSpec((1, d, tf), lambda c, t, te, nu: (te[t], 0, c)),
                      pl.BlockSpec((1, 1, tf), lambda c, t, te, nu: (te[t], 0, c)),
                      pl.BlockSpec((1, 1, tf), lambda c, t, te, nu: (te[t], 0, c))],
            out_specs=pl.BlockSpec((tm, tf), lambda c, t, te, nu: (t, c)),
            scratch_shapes=[pltpu.VMEM((d, tf), BF16), pltpu.VMEM((d, tf), BF16)]),
        out_shape=jax.ShapeDtypeStruct((p, f), BF16),
        name="expert_gate_up",
        compiler_params=_params("arbitrary", "arbitrary"),
    )(tile_expert, n_used, xs, w_gate, w_up, b_gate.reshape(ne, 1, f), b_up.reshape(ne, 1, f))
    return pl.pallas_call(
        _down_kernel,
        grid_spec=pltpu.PrefetchScalarGridSpec(
            num_scalar_prefetch=2,
            grid=(d // tn, nt),
            in_specs=[pl.BlockSpec((tm, f), lambda c, t, te, nu: (used(t, nu), 0)),
                      pl.BlockSpec((1, f, tn), lambda c, t, te, nu: (te[t], 0, c)),
                      pl.BlockSpec((1, 1, tn), lambda c, t, te, nu: (te[t], 0, c))],
            out_specs=pl.BlockSpec((tm, tn), lambda c, t, te, nu: (t, c)),
            scratch_shapes=[pltpu.VMEM((f, tn), BF16)]),
        out_shape=jax.ShapeDtypeStruct((p, d), F32),
        name="expert_down",
        compiler_params=_params("arbitrary", "arbitrary"),
    )(tile_expert, n_used, h, w_down, b_down.reshape(ne, 1, d))


def _combine_norm_kernel(slot_ref, x_ref, w_ref, g_ref, b_ref, ys_hbm, o_ref, buf, sems, *, tc):
    step = pl.program_id(0)
    cur = step % 2

    def fetch(s, half):
        base = s * tc * TOP_K

        def issue(r, carry):
            for k in range(TOP_K):
                _row_copy(ys_hbm, buf.at[half, k], slot_ref[base + r * TOP_K + k], r, sems.at[half]).start()
            return carry

        lax.fori_loop(0, tc, issue, 0, unroll=4)

    @pl.when(step == 0)
    def _():
        fetch(0, 0)

    @pl.when(step + 1 < pl.num_programs(0))
    def _():
        fetch(step + 1, 1 - cur)

    def drain(r, carry):
        for k in range(TOP_K):
            _row_copy(ys_hbm, buf.at[cur, k], 0, r, sems.at[cur]).wait()
        return carry

    lax.fori_loop(0, tc, drain, 0, unroll=4)

    w = w_ref[...]
    moe = w[:, 0:1] * buf[cur, 0]
    for k in range(1, TOP_K):
        moe = moe + w[:, k:k + 1] * buf[cur, k]
    o_ref[...] = _layer_norm(DEEPNORM_ALPHA * x_ref[...] + moe, g_ref[...], b_ref[...])


def _combine_norm(x, ys, slots, w4, g, b):
    t, d = x.shape
    tc = min(COMBINE_T, t)
    return pl.pallas_call(
        functools.partial(_combine_norm_kernel, tc=tc),
        grid_spec=pltpu.PrefetchScalarGridSpec(
            num_scalar_prefetch=1,
            grid=(t // tc,),
            in_specs=[pl.BlockSpec((tc, d), lambda i, s: (i, 0)),
                      pl.BlockSpec((tc, TOP_K), lambda i, s: (i, 0)),
                      pl.BlockSpec((1, d), lambda i, s: (0, 0)),
                      pl.BlockSpec((1, d), lambda i, s: (0, 0)),
                      pl.BlockSpec(memory_space=pl.ANY)],
            out_specs=pl.BlockSpec((tc, d), lambda i, s: (i, 0)),
            scratch_shapes=[pltpu.VMEM((2, TOP_K, tc, d), F32), pltpu.SemaphoreType.DMA((2,))]),
        out_shape=jax.ShapeDtypeStruct((t, d), F32),
        name="combine_norm",
        compiler_params=_params("arbitrary"),
    )(slots, x, w4, g.reshape(1, d), b.reshape(1, d), ys)


def _routing_tables(eid, pos, counts, tm):
    t = eid.shape[0]
    counts = counts[0]
    ne = counts.shape[0]
    padded = ((counts + tm - 1) // tm) * tm
    ends = jnp.cumsum(padded)
    starts = ends - padded
    n_tiles = (t * TOP_K) // tm + ne
    n_used = (ends[-1] // tm).astype(jnp.int32)
    tile_start = jnp.arange(n_tiles, dtype=jnp.int32) * tm
    tile_expert = jnp.sum((tile_start[:, None] >= ends[None, :]).astype(jnp.int32), axis=1)
    tile_expert = jnp.minimum(tile_expert, ne - 1)
    last = jnp.sum(jnp.where(jnp.arange(n_tiles) == n_used - 1, tile_expert, 0))
    tile_expert = jnp.where(jnp.arange(n_tiles) < n_used, tile_expert, last).astype(jnp.int32)
    group_start = jnp.sum(jnp.where(eid[:, :, None] == jnp.arange(ne)[None, None, :], starts[None, None, :], 0), axis=2)
    slots = (group_start + pos).astype(jnp.int32).reshape(-1)
    return slots, tile_expert, n_used.reshape(1), (starts + counts).astype(jnp.int32), ends.astype(jnp.int32), n_tiles * tm


def _post_norm_moe(x, h, ln1_g, ln1_b, router_w, router_b, w_gate, b_gate, w_up, b_up, w_down, b_down,
                   ln2_g, ln2_b):
    y, eid, w4, pos, counts = _norm_router(x, h, ln1_g, ln1_b, router_w, router_b)
    slots, tile_expert, n_used, pad_lo, pad_hi, n_rows = _routing_tables(eid, pos, counts, MOE_TM)
    xs = _dispatch_rows(y, slots, pad_lo, pad_hi, n_used, n_rows)
    ys = _expert_ffn(xs, tile_expert, n_used, w_gate, b_gate, w_up, b_up, w_down, b_down)
    return _combine_norm(y, ys, slots, w4, ln2_g, ln2_b)


def kernel(x, l0_w_in, l0_lambda_q1, l0_lambda_k1, l0_lambda_q2, l0_lambda_k2, l0_subln_g, l0_w_o, l0_ln1_g, l0_ln1_b, l0_router_w, l0_router_b, l0_w_gate, l0_b_gate, l0_w_up, l0_b_up, l0_w_down, l0_b_down, l0_ln2_g, l0_ln2_b, l1_w_in, l1_sinks, l1_w_o, l1_ln1_g, l1_ln1_b, l1_router_w, l1_router_b, l1_w_gate, l1_b_gate, l1_w_up, l1_b_up, l1_w_down, l1_b_down, l1_ln2_g, l1_ln2_b):
    batch, seq, d = x.shape
    xt = x.reshape(batch * seq, d)

    proj = _matmul(xt.astype(BF16), l0_w_in.astype(BF16), BF16)
    slopes = _alibi_slopes(N_DIFF_HEADS + N_MOBA_HEADS)
    lam_init = 0.8 - 0.6 * math.exp(-0.3 * 0)
    attn_a = _diff_attention(proj, l0_lambda_q1, l0_lambda_k1, l0_lambda_q2, l0_lambda_k2, l0_subln_g,
                             slopes[0::2], batch, seq, lam_init)
    attn_b = _moba_attention(proj, slopes[1::2], batch, seq)
    h = _matmul2(attn_a, attn_b, l0_w_o.astype(BF16), F32)
    xt = _post_norm_moe(xt, h, l0_ln1_g, l0_ln1_b, l0_router_w, l0_router_b, l0_w_gate, l0_b_gate,
                        l0_w_up, l0_b_up, l0_w_down, l0_b_down, l0_ln2_g, l0_ln2_b)

    proj = _matmul(xt.astype(BF16), l1_w_in.astype(BF16), BF16)
    q_w = N_SWA_HEADS * SWA_HEAD_DIM
    kv_w = N_SWA_KV_HEADS * SWA_HEAD_DIM
    dup = lambda a: jnp.broadcast_to(a.reshape(-1, N_SWA_KV_HEADS, 1, SWA_HEAD_DIM),
                                     (a.shape[0], N_SWA_KV_HEADS, 2, SWA_HEAD_DIM)).reshape(a.shape[0], 2 * kv_w)
    k2 = dup(proj[:, q_w:q_w + kv_w])
    v2 = dup(proj[:, q_w + kv_w:])
    attn = _swa_attention(proj, k2, v2, l1_sinks, _alibi_slopes(N_SWA_HEADS), batch, seq)
    h = _matmul(attn, l1_w_o.astype(BF16), F32)
    xt = _post_norm_moe(xt, h, l1_ln1_g, l1_ln1_b, l1_router_w, l1_router_b, l1_w_gate, l1_b_gate,
                        l1_w_up, l1_b_up, l1_w_down, l1_b_down, l1_ln2_g, l1_ln2_b)
    return xt.reshape(batch, seq, d)
```

```python
import functools
import math

import jax
import jax.numpy as jnp
from jax import lax
from jax.experimental import pallas as pl
from jax.experimental.pallas import tpu as pltpu

F32 = jnp.float32
BF16 = jnp.bfloat16
NEG_INF = float("-inf")

DEPTH = 2
DIFF_QK_DIM = 64
N_DIFF_HEADS = 16
N_MOBA_HEADS = 16
MOBA_BLOCK = 256
MOBA_TOPK = 3
N_SWA_HEADS = 64
N_SWA_KV_HEADS = 8
SWA_HEAD_DIM = 64
SWA_WINDOW = 128
N_EXPERTS = 32
TOP_K = 4
SWIGLU_LIMIT = 7.0
SWIGLU_ALPHA = 1.702
LN_EPS = 1e-5
SUBLN_EPS = 1e-5
DEEPNORM_ALPHA = (2.0 * DEPTH) ** 0.25
LOG2E = math.log2(math.e)

LANES = 128
VMEM_LIMIT = 56 * 1024 * 1024

MM_TM, MM_TN = 512, 1024
DIFF_T = 512
MOBA_T = 512
SWA_T = SWA_WINDOW
NORM_TM = 256
MOE_TM = 256
DISPATCH_T = 128
COMBINE_T = 128


def _params(*sem):
    return pltpu.CompilerParams(dimension_semantics=sem, vmem_limit_bytes=VMEM_LIMIT)


def _alibi_slopes(n):
    return 2.0 ** (-8.0 * (jnp.arange(n, dtype=F32) + 1.0) / n)


def _nt_dot(a, b):
    return lax.dot_general(a, b, (((1,), (1,)), ((), ())), preferred_element_type=F32)


def _matmul_kernel(x_ref, w_ref, o_ref):
    o_ref[...] = jnp.dot(x_ref[...], w_ref[...], preferred_element_type=F32).astype(o_ref.dtype)


def _scaled_matmul_kernel(x_ref, w_ref, s_ref, o_ref):
    acc = jnp.dot(x_ref[...], w_ref[...], preferred_element_type=F32)
    o_ref[...] = (acc * s_ref[...]).astype(o_ref.dtype)


def _matmul(x, w, out_dtype, col_scale=None, tm=MM_TM, tn=MM_TN):
    m, k = x.shape
    n = w.shape[1]
    tm, tn = min(tm, m), min(tn, n)
    in_specs = [pl.BlockSpec((tm, k), lambda j, i: (i, 0)), pl.BlockSpec((k, tn), lambda j, i: (0, j))]
    operands = [x, w]
    if col_scale is not None:
        in_specs.append(pl.BlockSpec((1, tn), lambda j, i: (0, j)))
        operands.append(col_scale.reshape(1, n))
    return pl.pallas_call(
        _matmul_kernel if col_scale is None else _scaled_matmul_kernel,
        grid=(n // tn, m // tm),
        in_specs=in_specs,
        out_specs=pl.BlockSpec((tm, tn), lambda j, i: (i, j)),
        out_shape=jax.ShapeDtypeStruct((m, n), out_dtype),
        name="matmul",
        compiler_params=_params("parallel", "parallel"),
    )(*operands)


def _matmul2_kernel(xa_ref, xb_ref, w_ref, o_ref, *, ka):
    acc = jnp.dot(xa_ref[...], w_ref[:ka, :], preferred_element_type=F32)
    acc = acc + jnp.dot(xb_ref[...], w_ref[ka:, :], preferred_element_type=F32)
    o_ref[...] = acc.astype(o_ref.dtype)


def _matmul2(xa, xb, w, out_dtype, tm=MM_TM, tn=MM_TN):
    m, ka = xa.shape
    kb = xb.shape[1]
    n = w.shape[1]
    tm, tn = min(tm, m), min(tn, n)
    return pl.pallas_call(
        functools.partial(_matmul2_kernel, ka=ka),
        grid=(n // tn, m // tm),
        in_specs=[pl.BlockSpec((tm, ka), lambda j, i: (i, 0)),
                  pl.BlockSpec((tm, kb), lambda j, i: (i, 0)),
                  pl.BlockSpec((ka + kb, tn), lambda j, i: (0, j))],
        out_specs=pl.BlockSpec((tm, tn), lambda j, i: (i, j)),
        out_shape=jax.ShapeDtypeStruct((m, n), out_dtype),
        name="matmul2",
        compiler_params=_params("parallel", "parallel"),
    )(xa, xb, w)


def _softmax_step(m_ref, l_ref, a_ref, st, shift, vt):
    m_prev = m_ref[...]
    m_next = jnp.maximum(m_prev, jnp.max(st, axis=0, keepdims=True) - shift)
    alpha = jnp.exp2(m_prev - m_next)
    p = jnp.exp2(st - (m_next + shift))
    l_ref[...] = alpha * l_ref[...] + jnp.sum(p, axis=0, keepdims=True)
    a_ref[...] = alpha * a_ref[...] + jnp.dot(vt, p.astype(BF16), preferred_element_type=F32)
    m_ref[...] = m_next


def _softmax_init(m_ref, l_ref, a_ref):
    m_ref[...] = jnp.full(m_ref.shape, NEG_INF, F32)
    l_ref[...] = jnp.zeros(l_ref.shape, F32)
    a_ref[...] = jnp.zeros(a_ref.shape, F32)


def _cast_specs(weights, steps, step_of):
    in_specs, out_specs, out_shapes = [], [], []
    for w in weights:
        rows, cols = w.shape
        blk = rows // steps
        assert blk * steps == rows and blk % 16 == 0, (w.shape, steps)
        spec = pl.BlockSpec((blk, cols), lambda *ids: (step_of(*ids), 0))
        in_specs.append(spec)
        out_specs.append(spec)
        out_shapes.append(jax.ShapeDtypeStruct(w.shape, BF16))
    return in_specs, out_specs, out_shapes


def _cast_blocks(src_refs, dst_refs):
    for src, dst in zip(src_refs, dst_refs):
        dst[...] = src[...].astype(dst.dtype)


def _diff_attn_kernel(*refs, t, lam_init, n_cast):
    (slopes_ref, lq1_ref, lk1_ref, lq2_ref, lk2_ref, g_ref, q_ref, k_ref, vt_ref), refs = refs[:9], refs[9:]
    cast_src, (o_ref,), cast_dst = refs[:n_cast], refs[n_cast:n_cast + 1], refs[n_cast + 1:2 * n_cast + 1]
    m1, l1, a1, m2, l2, a2 = refs[2 * n_cast + 1:]
    _cast_blocks(cast_src, cast_dst)

    h = pl.program_id(1)
    i = pl.program_id(2)
    slope2 = slopes_ref[h] * LOG2E
    lam = (jnp.exp(jnp.sum(lq1_ref[...] * lk1_ref[...], keepdims=True))
           - jnp.exp(jnp.sum(lq2_ref[...] * lk2_ref[...], keepdims=True)) + lam_init)

    q = q_ref[...]
    lane = lax.broadcasted_iota(jnp.int32, q.shape, 1)
    q1 = jnp.where(lane < DIFF_QK_DIM, q, jnp.zeros_like(q))
    q2 = jnp.where(lane >= DIFF_QK_DIM, q, jnp.zeros_like(q))

    rel = lax.broadcasted_iota(jnp.int32, (t, t), 1) - lax.broadcasted_iota(jnp.int32, (t, t), 0)
    bias0 = -slope2 * rel.astype(F32)

    _softmax_init(m1, l1, a1)
    _softmax_init(m2, l2, a2)

    def tile(j, shift, causal):
        kt = k_ref[pl.ds(pl.multiple_of(j * t, t), t), :]
        vt = vt_ref[j]
        for qz, (m_r, l_r, a_r) in ((q1, (m1, l1, a1)), (q2, (m2, l2, a2))):
            st = _nt_dot(kt, qz) + bias0
            if causal:
                st = jnp.where(rel >= 0, st, NEG_INF)
            _softmax_step(m_r, l_r, a_r, st, shift, vt)

    tile(i, 0.0, True)

    def body(j, carry):
        tile(j, slope2 * ((i - j) * t).astype(F32), False)
        return carry

    lax.fori_loop(0, i, body, 0)

    ot = a1[...] / l1[...] - lam * (a2[...] / l2[...])
    ot = ot * lax.rsqrt(jnp.mean(ot * ot, axis=0, keepdims=True) + SUBLN_EPS)
    o_ref[...] = (ot.T * g_ref[...] * (1.0 - lam_init)).astype(o_ref.dtype)


def _tile_transposed_values(v, batch, seq, nh, t):
    v = v.reshape(batch, seq // t, t, nh, LANES).transpose(0, 3, 1, 4, 2)
    return v.reshape(batch * nh, seq // t, LANES, t)


def _diff_attention(proj, lq1, lk1, lq2, lk2, subln_g, slopes, batch, seq, lam_init, cast_weights):
    t = min(DIFF_T, seq)
    nq = seq // t
    nh = N_DIFF_HEADS
    vt = _tile_transposed_values(proj[:, 2 * nh * LANES:3 * nh * LANES], batch, seq, nh, t)
    vec = lambda n: pl.BlockSpec((1, n), lambda b, h, i: (0, 0))
    c_in, c_out, c_shapes = _cast_specs(cast_weights, batch * nh * nq, lambda b, h, i: (b * nh + h) * nq + i)
    kern = functools.partial(_diff_attn_kernel, t=t, lam_init=lam_init, n_cast=len(cast_weights))
    stat = pltpu.VMEM((1, t), F32)
    acc = pltpu.VMEM((LANES, t), F32)
    out = pl.pallas_call(
        kern,
        grid=(batch, nh, nq),
        in_specs=[pl.BlockSpec(memory_space=pltpu.SMEM),
                  vec(DIFF_QK_DIM), vec(DIFF_QK_DIM), vec(DIFF_QK_DIM), vec(DIFF_QK_DIM), vec(LANES),
                  pl.BlockSpec((t, LANES), lambda b, h, i: (b * nq + i, h)),
                  pl.BlockSpec((seq, LANES), lambda b, h, i: (b, nh + h)),
                  pl.BlockSpec((None, nq, LANES, t), lambda b, h, i: (b * nh + h, 0, 0, 0))] + c_in,
        out_specs=[pl.BlockSpec((t, LANES), lambda b, h, i: (b * nq + i, h))] + c_out,
        out_shape=[jax.ShapeDtypeStruct((batch * seq, nh * LANES), BF16)] + c_shapes,
        scratch_shapes=[stat, stat, acc, stat, stat, acc],
        name="diff_attention",
        compiler_params=_params("parallel", "parallel", "parallel"),
    )(slopes, lq1.reshape(1, -1), lk1.reshape(1, -1), lq2.reshape(1, -1), lk2.reshape(1, -1),
      subln_g.reshape(1, -1), proj, proj, vt, *cast_weights)
    return out[0], out[1:]


def _moba_kernel(*refs, nb, t, n_cast):
    (slopes_ref, q_ref, k_ref, vt_ref), refs = refs[:4], refs[4:]
    cast_src, (o_ref,), cast_dst = refs[:n_cast], refs[n_cast:n_cast + 1], refs[n_cast + 1:2 * n_cast + 1]
    km_hi, km_mid, km_lo, selb, m_r, l_r, a_r = refs[2 * n_cast + 1:]
    _cast_blocks(cast_src, cast_dst)

    blk = MOBA_BLOCK
    per = t // blk
    h = pl.program_id(1)
    i = pl.program_id(2)
    slope2 = slopes_ref[h] * LOG2E

    @pl.when(i == 0)
    def _():
        km = jnp.sum(k_ref[...].astype(F32).reshape(nb, blk, LANES), axis=1) * (1.0 / blk)
        hi = km.astype(BF16)
        r1 = km - hi.astype(F32)
        mid = r1.astype(BF16)
        km_hi[...] = hi
        km_mid[...] = mid
        km_lo[...] = (r1 - mid.astype(F32)).astype(BF16)

    q = q_ref[...]
    gate = _nt_dot(km_hi[...], q) + _nt_dot(km_mid[...], q) + _nt_dot(km_lo[...], q)
    blk_id = lax.broadcasted_iota(jnp.int32, gate.shape, 0)
    q_sub = lax.broadcasted_iota(jnp.int32, (1, t), 1) // blk
    past = blk_id < i * per + q_sub
    gate = jnp.where(past, gate, NEG_INF)
    beaten = jnp.zeros(gate.shape, jnp.int32)
    for n in range(nb):
        gn = gate[n:n + 1, :]
        beaten = beaten + jnp.where(gn > gate, 1, jnp.where(gn == gate, jnp.where(blk_id > n, 1, 0), 0))
    selb[...] = jnp.where(past, jnp.where(beaten < MOBA_TOPK, 0.0, NEG_INF), NEG_INF)

    rel = lax.broadcasted_iota(jnp.int32, (t, t), 1) - lax.broadcasted_iota(jnp.int32, (t, t), 0)
    bias0 = -slope2 * rel.astype(F32)
    _softmax_init(m_r, l_r, a_r)

    def tile(j, shift, own):
        rows = []
        for u in range(per):
            row = selb[pl.ds(j * per + u, 1), :]
            if own:
                row = jnp.where(q_sub <= u, 0.0, row)
            rows.append(jnp.broadcast_to(row, (blk, t)))
        st = _nt_dot(k_ref[pl.ds(pl.multiple_of(j * t, t), t), :], q) + bias0 + jnp.concatenate(rows, axis=0)
        if own:
            st = jnp.where(rel >= 0, st, NEG_INF)
        _softmax_step(m_r, l_r, a_r, st, shift, vt_ref[j])

    tile(i, 0.0, True)

    def body(j, carry):
        tile(j, slope2 * ((i - j) * t).astype(F32), False)
        return carry

    lax.fori_loop(0, i, body, 0)
    o_ref[...] = (a_r[...] / l_r[...]).T.astype(o_ref.dtype)


def _moba_attention(proj, slopes, batch, seq, cast_weights):
    blk = MOBA_BLOCK
    t = min(MOBA_T, seq)
    nq = seq // t
    nb = seq // blk
    nh = N_MOBA_HEADS
    base = 3 * N_DIFF_HEADS
    vt = _tile_transposed_values(proj[:, (base + 2 * nh) * LANES:(base + 3 * nh) * LANES], batch, seq, nh, t)
    c_in, c_out, c_shapes = _cast_specs(cast_weights, batch * nh * nq, lambda b, h, i: (b * nh + h) * nq + i)
    kern = functools.partial(_moba_kernel, nb=nb, t=t, n_cast=len(cast_weights))
    out = pl.pallas_call(
        kern,
        grid=(batch, nh, nq),
        in_specs=[pl.BlockSpec(memory_space=pltpu.SMEM),
                  pl.BlockSpec((t, LANES), lambda b, h, i: (b * nq + i, base + h)),
                  pl.BlockSpec((seq, LANES), lambda b, h, i: (b, base + nh + h)),
                  pl.BlockSpec((None, nq, LANES, t), lambda b, h, i: (b * nh + h, 0, 0, 0))] + c_in,
        out_specs=[pl.BlockSpec((t, LANES), lambda b, h, i: (b * nq + i, h))] + c_out,
        out_shape=[jax.ShapeDtypeStruct((batch * seq, nh * LANES), BF16)] + c_shapes,
        scratch_shapes=[pltpu.VMEM((nb, LANES), BF16), pltpu.VMEM((nb, LANES), BF16), pltpu.VMEM((nb, LANES), BF16),
                        pltpu.VMEM((nb, t), F32),
                        pltpu.VMEM((1, t), F32), pltpu.VMEM((1, t), F32), pltpu.VMEM((LANES, t), F32)],
        name="moba_attention",
        compiler_params=_params("parallel", "parallel", "arbitrary"),
    )(slopes, proj, proj, vt, *cast_weights)
    return out[0], out[1:]


def _swa_kernel(slopes_ref, sinks_ref, q_ref, kp_ref, kc_ref, vp_ref, vc_ref, o_ref, *, group):
    w = SWA_WINDOW
    n = pl.program_id(1)
    g = pl.program_id(2)
    kcat = jnp.concatenate([kp_ref[...], kc_ref[...]], axis=0)
    vcat = jnp.concatenate([vp_ref[...], vc_ref[...]], axis=0)
    r = lax.broadcasted_iota(jnp.int32, (w, 2 * w), 0)
    c = lax.broadcasted_iota(jnp.int32, (w, 2 * w), 1)
    dist = r - c + w
    first_key = jnp.where(n > 0, 0, w)
    ok = jnp.logical_and(jnp.logical_and(dist >= 0, dist < w), c >= first_key)
    distf = dist.astype(F32)
    lane = lax.broadcasted_iota(jnp.int32, (w, LANES), 1)
    low = lane < SWA_HEAD_DIM
    for pair in range(group // 2):
        qp = q_ref[:, pair * LANES:(pair + 1) * LANES]
        outs = []
        for half in range(2):
            head = g * group + pair * 2 + half
            slope2 = slopes_ref[head] * LOG2E
            sink2 = sinks_ref[head] * LOG2E
            qz = jnp.where(low if half == 0 else jnp.logical_not(low), qp, jnp.zeros_like(qp))
            s = _nt_dot(qz, kcat) - slope2 * distf
            s = jnp.where(ok, s, NEG_INF)
            m = jnp.maximum(jnp.max(s, axis=1, keepdims=True), sink2)
            e = jnp.exp2(s - m)
            den = jnp.sum(e, axis=1, keepdims=True) + jnp.exp2(sink2 - m)
            outs.append(jnp.dot(e.astype(BF16), vcat, preferred_element_type=F32) / den)
        o_ref[:, pair * LANES:(pair + 1) * LANES] = jnp.where(low, outs[0], outs[1]).astype(o_ref.dtype)


def _swa_attention(proj, k2, v2, sinks, slopes, batch, seq):
    w = SWA_WINDOW
    nblk = seq // w
    group = N_SWA_HEADS // N_SWA_KV_HEADS
    gw = group * SWA_HEAD_DIM
    prev = lambda b, n, g: (b * nblk + jnp.maximum(n - 1, 0), g)
    cur = lambda b, n, g: (b * nblk + n, g)
    return pl.pallas_call(
        functools.partial(_swa_kernel, group=group),
        grid=(batch, nblk, N_SWA_KV_HEADS),
        in_specs=[pl.BlockSpec(memory_space=pltpu.SMEM), pl.BlockSpec(memory_space=pltpu.SMEM),
                  pl.BlockSpec((w, gw), cur),
                  pl.BlockSpec((w, LANES), prev), pl.BlockSpec((w, LANES), cur),
                  pl.BlockSpec((w, LANES), prev), pl.BlockSpec((w, LANES), cur)],
        out_specs=pl.BlockSpec((w, gw), cur),
        out_shape=jax.ShapeDtypeStruct((batch * seq, N_SWA_HEADS * SWA_HEAD_DIM), BF16),
        name="swa_attention",
        compiler_params=_params("parallel", "parallel", "parallel"),
    )(slopes, sinks, proj, k2, k2, v2, v2)


def _layer_norm(z, g, b):
    mu = jnp.mean(z, axis=-1, keepdims=True)
    zc = z - mu
    var = jnp.mean(zc * zc, axis=-1, keepdims=True)
    return zc * lax.rsqrt(var + LN_EPS) * g + b


def _split3(a):
    hi = a.astype(BF16)
    r1 = a - hi.astype(F32)
    mid = r1.astype(BF16)
    return hi, mid, (r1 - mid.astype(F32)).astype(BF16)


def _norm_router_kernel(x_ref, h_ref, g_ref, b_ref, rw_ref, rb_ref,
                        y_ref, eid_ref, w_ref, pos_ref, cnt_ref, carry, *, tm):
    step = pl.program_id(0)

    @pl.when(step == 0)
    def _():
        carry[...] = jnp.zeros(carry.shape, F32)

    y = _layer_norm(DEEPNORM_ALPHA * x_ref[...] + h_ref[...], g_ref[...], b_ref[...])
    y_ref[...] = y

    yh, ym, yl = _split3(y)
    wh, wm, wl = _split3(rw_ref[...])
    dot = lambda a, b: jnp.dot(a, b, preferred_element_type=F32)
    logits = (dot(yl, wh) + dot(ym, wm) + dot(yh, wl)) + (dot(ym, wh) + dot(yh, wm)) + dot(yh, wh) + rb_ref[...]

    e_id = lax.broadcasted_iota(jnp.int32, logits.shape, 1)
    beaten = jnp.zeros(logits.shape, jnp.int32)
    for n in range(N_EXPERTS):
        ln = logits[:, n:n + 1]
        beaten = beaten + jnp.where(ln > logits, 1, jnp.where(ln == logits, jnp.where(e_id > n, 1, 0), 0))
    sel = beaten < TOP_K
    ex = jnp.where(sel, jnp.exp(logits - jnp.max(logits, axis=1, keepdims=True)), 0.0)
    weight = ex / jnp.sum(ex, axis=1, keepdims=True)

    self = jnp.where(sel, 1.0, 0.0)
    rr = lax.broadcasted_iota(jnp.int32, (tm, tm), 0)
    cc = lax.broadcasted_iota(jnp.int32, (tm, tm), 1)
    before = jnp.where(rr > cc, 1.0, 0.0).astype(BF16)
    pos = carry[...] + jnp.dot(before, self.astype(BF16), preferred_element_type=F32)
    carry[...] = carry[...] + jnp.sum(self, axis=0, keepdims=True)
    cnt_ref[...] = carry[...].astype(jnp.int32)

    e_f = e_id.astype(F32)
    k_id = lax.broadcasted_iota(jnp.int32, (tm, TOP_K), 1)
    e_k = jnp.zeros((tm, TOP_K), F32)
    w_k = jnp.zeros((tm, TOP_K), F32)
    p_k = jnp.zeros((tm, TOP_K), F32)
    for k in range(TOP_K):
        hit = beaten == k
        pick = lambda a: jnp.sum(jnp.where(hit, a, 0.0), axis=1, keepdims=True)
        e_k = jnp.where(k_id == k, pick(e_f), e_k)
        w_k = jnp.where(k_id == k, pick(weight), w_k)
        p_k = jnp.where(k_id == k, pick(pos), p_k)
    eid_ref[...] = e_k.astype(jnp.int32)
    w_ref[...] = w_k
    pos_ref[...] = p_k.astype(jnp.int32)


def _norm_router(x, h, g, b, rw, rb):
    t, d = x.shape
    tm = min(NORM_TM, t)
    ne = rw.shape[1]
    row = pl.BlockSpec((tm, d), lambda i: (i, 0))
    vec = pl.BlockSpec((1, d), lambda i: (0, 0))
    per_k = pl.BlockSpec((tm, TOP_K), lambda i: (i, 0))
    return pl.pallas_call(
        functools.partial(_norm_router_kernel, tm=tm),
        grid=(t // tm,),
        in_specs=[row, row, vec, vec, pl.BlockSpec((d, ne), lambda i: (0, 0)), pl.BlockSpec((1, ne), lambda i: (0, 0))],
        out_specs=[row, per_k, per_k, per_k, pl.BlockSpec((1, ne), lambda i: (0, 0))],
        out_shape=[jax.ShapeDtypeStruct((t, d), F32), jax.ShapeDtypeStruct((t, TOP_K), jnp.int32),
                   jax.ShapeDtypeStruct((t, TOP_K), F32), jax.ShapeDtypeStruct((t, TOP_K), jnp.int32),
                   jax.ShapeDtypeStruct((1, ne), jnp.int32)],
        scratch_shapes=[pltpu.VMEM((1, ne), F32)],
        name="norm_router",
        compiler_params=_params("arbitrary"),
    )(x, h, g.reshape(1, d), b.reshape(1, d), rw, rb.reshape(1, ne))


def _row_copy(src_ref, dst_ref, src_row, dst_row, sem):
    return pltpu.make_async_copy(src_ref.at[pl.ds(src_row, 1), :], dst_ref.at[pl.ds(dst_row, 1), :], sem)


def _dispatch_kernel(slot_ref, pad_lo_ref, pad_hi_ref, nu_ref, y_ref, xs_hbm, zrow, sem, pad_sem, *, tm, ne):
    step = pl.program_id(0)

    @pl.when(step == 0)
    def _():
        zrow[...] = jnp.zeros(zrow.shape, zrow.dtype)
        tile_rows = zrow.shape[0]
        tail = lambda tl: pltpu.make_async_copy(
            zrow, xs_hbm.at[pl.ds(pl.multiple_of(tl * tile_rows, tile_rows), tile_rows), :], pad_sem)

        def fill_tile(tl, carry):
            tail(tl).start()
            return carry

        def tile_filled(tl, carry):
            tail(tl).wait()
            return carry

        lax.fori_loop(nu_ref[0], xs_hbm.shape[0] // tile_rows, fill_tile, 0)
        lax.fori_loop(nu_ref[0], xs_hbm.shape[0] // tile_rows, tile_filled, 0)
        for e in range(ne):
            lo = pad_lo_ref[e]
            hi = pad_hi_ref[e]

            def fill(r, carry):
                _row_copy(zrow, xs_hbm, 0, r, pad_sem).start()
                return carry

            def filled(r, carry):
                _row_copy(zrow, xs_hbm, 0, r, pad_sem).wait()
                return carry

            lax.fori_loop(lo, hi, fill, 0)
            lax.fori_loop(lo, hi, filled, 0)

    base = step * tm * TOP_K

    def issue(r, carry):
        for k in range(TOP_K):
            _row_copy(y_ref, xs_hbm, r, slot_ref[base + r * TOP_K + k], sem).start()
        return carry

    def drain(r, carry):
        for k in range(TOP_K):
            _row_copy(y_ref, xs_hbm, r, slot_ref[base + r * TOP_K + k], sem).wait()
        return carry

    lax.fori_loop(0, tm, issue, 0, unroll=4)
    lax.fori_loop(0, tm, drain, 0, unroll=4)


def _dispatch_rows(y, slots, pad_lo, pad_hi, n_used, n_rows):
    t, d = y.shape
    tm = min(DISPATCH_T, t)
    ne = pad_lo.shape[0]
    return pl.pallas_call(
        functools.partial(_dispatch_kernel, tm=tm, ne=ne),
        grid_spec=pltpu.PrefetchScalarGridSpec(
            num_scalar_prefetch=4,
            grid=(t // tm,),
            in_specs=[pl.BlockSpec((tm, d), lambda i, s, lo, hi, nu: (i, 0))],
            out_specs=pl.BlockSpec(memory_space=pl.ANY),
            scratch_shapes=[pltpu.VMEM((MOE_TM, d), y.dtype), pltpu.SemaphoreType.DMA(()),
                            pltpu.SemaphoreType.DMA(())]),
        out_shape=jax.ShapeDtypeStruct((n_rows, d), y.dtype),
        name="dispatch_scatter",
        compiler_params=_params("arbitrary"),
    )(slots, pad_lo, pad_hi, n_used, y)


def _gate_up_kernel(te_ref, nu_ref, x_ref, wg_ref, wu_ref, bg_ref, bu_ref, h_ref):
    t = pl.program_id(0)

    @pl.when(t < nu_ref[0])
    def _():
        x = x_ref[...].astype(BF16)
        gate = jnp.minimum(jnp.dot(x, wg_ref[0], preferred_element_type=F32) + bg_ref[0], SWIGLU_LIMIT)
        up = jnp.clip(jnp.dot(x, wu_ref[0], preferred_element_type=F32) + bu_ref[0], -SWIGLU_LIMIT, SWIGLU_LIMIT)
        h_ref[...] = ((up + 1.0) * gate * jax.nn.sigmoid(SWIGLU_ALPHA * gate)).astype(h_ref.dtype)

    @pl.when(t >= nu_ref[0])
    def _():
        h_ref[...] = jnp.zeros(h_ref.shape, h_ref.dtype)


def _down_kernel(te_ref, nu_ref, h_ref, wd_ref, bd_ref, y_ref):
    t = pl.program_id(0)

    @pl.when(t < nu_ref[0])
    def _():
        y_ref[...] = jnp.dot(h_ref[...], wd_ref[0], preferred_element_type=F32) + bd_ref[0]

    @pl.when(t >= nu_ref[0])
    def _():
        y_ref[...] = jnp.zeros(y_ref.shape, y_ref.dtype)


def _expert_ffn(xs, tile_expert, n_used, w_gate, b_gate, w_up, b_up, w_down, b_down):
    p, d = xs.shape
    ne, _, f = w_gate.shape
    tm = MOE_TM
    nt = p // tm
    used = lambda t, nu: jnp.minimum(t, nu[0] - 1)
    expert = lambda t, te, nu: (te[t], 0, 0)
    h = pl.pallas_call(
        _gate_up_kernel,
        grid_spec=pltpu.PrefetchScalarGridSpec(
            num_scalar_prefetch=2,
            grid=(nt,),
            in_specs=[pl.BlockSpec((tm, d), lambda t, te, nu: (used(t, nu), 0)),
                      pl.BlockSpec((1, d, f), expert), pl.BlockSpec((1, d, f), expert),
                      pl.BlockSpec((1, 1, f), expert), pl.BlockSpec((1, 1, f), expert)],
            out_specs=pl.BlockSpec((tm, f), lambda t, te, nu: (t, 0))),
        out_shape=jax.ShapeDtypeStruct((p, f), BF16),
        name="expert_gate_up",
        compiler_params=_params("arbitrary"),
    )(tile_expert, n_used, xs, w_gate, w_up, b_gate.reshape(ne, 1, f), b_up.reshape(ne, 1, f))
    return pl.pallas_call(
        _down_kernel,
        grid_spec=pltpu.PrefetchScalarGridSpec(
            num_scalar_prefetch=2,
            grid=(nt,),
            in_specs=[pl.BlockSpec((tm, f), lambda t, te, nu: (used(t, nu), 0)),
                      pl.BlockSpec((1, f, d), expert), pl.BlockSpec((1, 1, d), expert)],
            out_specs=pl.BlockSpec((tm, d), lambda t, te, nu: (t, 0))),
        out_shape=jax.ShapeDtypeStruct((p, d), F32),
        name="expert_down",
        compiler_params=_params("arbitrary"),
    )(tile_expert, n_used, h, w_down, b_down.reshape(ne, 1, d))


def _combine_norm_kernel(slot_ref, x_ref, w_ref, g_ref, b_ref, ys_hbm, o_ref, obf_ref, buf, sems, *, tc):
    step = pl.program_id(0)
    cur = step % 2

    def fetch(s, half):
        base = s * tc * TOP_K

        def issue(r, carry):
            for k in range(TOP_K):
                _row_copy(ys_hbm, buf.at[half, k], slot_ref[base + r * TOP_K + k], r, sems.at[half]).start()
            return carry

        lax.fori_loop(0, tc, issue, 0, unroll=4)

    @pl.when(step == 0)
    def _():
        fetch(0, 0)

    @pl.when(step + 1 < pl.num_programs(0))
    def _():
        fetch(step + 1, 1 - cur)

    def drain(r, carry):
        for k in range(TOP_K):
            _row_copy(ys_hbm, buf.at[cur, k], 0, r, sems.at[cur]).wait()
        return carry

    lax.fori_loop(0, tc, drain, 0, unroll=4)

    w = w_ref[...]
    moe = w[:, 0:1] * buf[cur, 0]
    for k in range(1, TOP_K):
        moe = moe + w[:, k:k + 1] * buf[cur, k]
    out = _layer_norm(DEEPNORM_ALPHA * x_ref[...] + moe, g_ref[...], b_ref[...])
    o_ref[...] = out
    obf_ref[...] = out.astype(obf_ref.dtype)


def _combine_norm(x, ys, slots, w4, g, b):
    t, d = x.shape
    tc = min(COMBINE_T, t)
    row = pl.BlockSpec((tc, d), lambda i, s: (i, 0))
    return pl.pallas_call(
        functools.partial(_combine_norm_kernel, tc=tc),
        grid_spec=pltpu.PrefetchScalarGridSpec(
            num_scalar_prefetch=1,
            grid=(t // tc,),
            in_specs=[row,
                      pl.BlockSpec((tc, TOP_K), lambda i, s: (i, 0)),
                      pl.BlockSpec((1, d), lambda i, s: (0, 0)),
                      pl.BlockSpec((1, d), lambda i, s: (0, 0)),
                      pl.BlockSpec(memory_space=pl.ANY)],
            out_specs=[row, row],
            scratch_shapes=[pltpu.VMEM((2, TOP_K, tc, d), F32), pltpu.SemaphoreType.DMA((2,))]),
        out_shape=[jax.ShapeDtypeStruct((t, d), F32), jax.ShapeDtypeStruct((t, d), BF16)],
        name="combine_norm",
        compiler_params=_params("arbitrary"),
    )(slots, x, w4, g.reshape(1, d), b.reshape(1, d), ys)


def _routing_tables(eid, pos, counts, tm):
    t = eid.shape[0]
    counts = counts[0]
    ne = counts.shape[0]
    padded = ((counts + tm - 1) // tm) * tm
    ends = jnp.cumsum(padded)
    starts = ends - padded
    n_tiles = (t * TOP_K) // tm + ne
    n_used = (ends[-1] // tm).astype(jnp.int32)
    tile_start = jnp.arange(n_tiles, dtype=jnp.int32) * tm
    tile_expert = jnp.sum((tile_start[:, None] >= ends[None, :]).astype(jnp.int32), axis=1)
    tile_expert = jnp.minimum(tile_expert, ne - 1)
    last = jnp.sum(jnp.where(jnp.arange(n_tiles) == n_used - 1, tile_expert, 0))
    tile_expert = jnp.where(jnp.arange(n_tiles) < n_used, tile_expert, last).astype(jnp.int32)
    group_start = jnp.sum(jnp.where(eid[:, :, None] == jnp.arange(ne)[None, None, :], starts[None, None, :], 0), axis=2)
    slots = (group_start + pos).astype(jnp.int32).reshape(-1)
    return slots, tile_expert, n_used.reshape(1), (starts + counts).astype(jnp.int32), ends.astype(jnp.int32), n_tiles * tm


def _post_norm_moe(x, h, ln1_g, ln1_b, router_w, router_b, w_gate, b_gate, w_up, b_up, w_down, b_down,
                   ln2_g, ln2_b):
    y, eid, w4, pos, counts = _norm_router(x, h, ln1_g, ln1_b, router_w, router_b)
    slots, tile_expert, n_used, pad_lo, pad_hi, n_rows = _routing_tables(eid, pos, counts, MOE_TM)
    xs = _dispatch_rows(y, slots, pad_lo, pad_hi, n_used, n_rows)
    ys = _expert_ffn(xs, tile_expert, n_used, w_gate, b_gate, w_up, b_up, w_down, b_down)
    return _combine_norm(y, ys, slots, w4, ln2_g, ln2_b)


def kernel(x, l0_w_in, l0_lambda_q1, l0_lambda_k1, l0_lambda_q2, l0_lambda_k2, l0_subln_g, l0_w_o, l0_ln1_g, l0_ln1_b, l0_router_w, l0_router_b, l0_w_gate, l0_b_gate, l0_w_up, l0_b_up, l0_w_down, l0_b_down, l0_ln2_g, l0_ln2_b, l1_w_in, l1_sinks, l1_w_o, l1_ln1_g, l1_ln1_b, l1_router_w, l1_router_b, l1_w_gate, l1_b_gate, l1_w_up, l1_b_up, l1_w_down, l1_b_down, l1_ln2_g, l1_ln2_b):
    batch, seq, d = x.shape
    xt = x.reshape(batch * seq, d)
    flat = lambda w: w.reshape(-1, w.shape[-1])
    like = lambda wb, w: wb.reshape(w.shape)

    a_w = N_DIFF_HEADS * 2 * DIFF_QK_DIM
    b_w = N_MOBA_HEADS * LANES
    one = lambda n: jnp.ones((n,), F32)
    scale0 = jnp.concatenate([one(a_w) * (LOG2E * DIFF_QK_DIM ** -0.5), one(2 * a_w),
                              one(b_w) * (LOG2E * LANES ** -0.5), one(2 * b_w)])
    proj = _matmul(xt.astype(BF16), l0_w_in.astype(BF16), BF16, col_scale=scale0)
    slopes = _alibi_slopes(N_DIFF_HEADS + N_MOBA_HEADS)
    lam_init = 0.8 - 0.6 * math.exp(-0.3 * 0)
    attn_a, (wg0, wu0, wd0, wo0) = _diff_attention(
        proj, l0_lambda_q1, l0_lambda_k1, l0_lambda_q2, l0_lambda_k2, l0_subln_g, slopes[0::2], batch, seq,
        lam_init, [flat(l0_w_gate), flat(l0_w_up), flat(l0_w_down), l0_w_o])
    attn_b, (wg1, wu1, wd1, wi1, wo1) = _moba_attention(
        proj, slopes[1::2], batch, seq, [flat(l1_w_gate), flat(l1_w_up), flat(l1_w_down), l1_w_in, l1_w_o])
    h = _matmul2(attn_a, attn_b, wo0, F32)
    xt, xt_bf = _post_norm_moe(xt, h, l0_ln1_g, l0_ln1_b, l0_router_w, l0_router_b, like(wg0, l0_w_gate), l0_b_gate,
                               like(wu0, l0_w_up), l0_b_up, like(wd0, l0_w_down), l0_b_down, l0_ln2_g, l0_ln2_b)

    q_w = N_SWA_HEADS * SWA_HEAD_DIM
    kv_w = N_SWA_KV_HEADS * SWA_HEAD_DIM
    scale1 = jnp.concatenate([one(q_w) * (LOG2E * SWA_HEAD_DIM ** -0.5), one(2 * kv_w)])
    proj = _matmul(xt_bf, wi1, BF16, col_scale=scale1)
    dup = lambda a: jnp.broadcast_to(a.reshape(-1, N_SWA_KV_HEADS, 1, SWA_HEAD_DIM),
                                     (a.shape[0], N_SWA_KV_HEADS, 2, SWA_HEAD_DIM)).reshape(a.shape[0], 2 * kv_w)
    k2 = dup(proj[:, q_w:q_w + kv_w])
    v2 = dup(proj[:, q_w + kv_w:])
    attn = _swa_attention(proj, k2, v2, l1_sinks, _alibi_slopes(N_SWA_HEADS), batch, seq)
    h = _matmul(attn, wo1, F32)
    xt, _ = _post_norm_moe(xt, h, l1_ln1_g, l1_ln1_b, l1_router_w, l1_router_b, like(wg1, l1_w_gate), l1_b_gate,
                           like(wu1, l1_w_up), l1_b_up, like(wd1, l1_w_down), l1_b_down, l1_ln2_g, l1_ln2_b)
    return xt.reshape(batch, seq, d)
```

```python
import functools
import math

import jax
import jax.numpy as jnp
from jax import lax
from jax.experimental import pallas as pl
from jax.experimental.pallas import tpu as pltpu

F32 = jnp.float32
BF16 = jnp.bfloat16
NEG_INF = float("-inf")
MASKED = -1e30

DEPTH = 2
DIFF_QK_DIM = 64
N_DIFF_HEADS = 16
N_MOBA_HEADS = 16
MOBA_BLOCK = 256
MOBA_TOPK = 3
N_SWA_HEADS = 64
N_SWA_KV_HEADS = 8
SWA_HEAD_DIM = 64
SWA_WINDOW = 128
N_EXPERTS = 32
TOP_K = 4
SWIGLU_LIMIT = 7.0
SWIGLU_ALPHA = 1.702
LN_EPS = 1e-5
SUBLN_EPS = 1e-5
DEEPNORM_ALPHA = (2.0 * DEPTH) ** 0.25
LOG2E = math.log2(math.e)

LANES = 128
BF16_SUBLANES = 16
DV_ROWS = LANES + BF16_SUBLANES
VMEM_LIMIT = 56 * 1024 * 1024

MM_TM, MM_TN = 512, 1024
DIFF_T = 512
MOBA_T = 512
SOFTMAX_STRIP = 32
SWA_T = SWA_WINDOW
NORM_TM = 256
MOE_TM = 256
DISPATCH_T = 128
COMBINE_T = 128


def _params(*sem):
    return pltpu.CompilerParams(dimension_semantics=sem, vmem_limit_bytes=VMEM_LIMIT)


def _alibi_slopes(n):
    return 2.0 ** (-8.0 * (jnp.arange(n, dtype=F32) + 1.0) / n)


def _nt_dot(a, b):
    return lax.dot_general(a, b, (((1,), (1,)), ((), ())), preferred_element_type=F32)


def _matmul_kernel(x_ref, w_ref, o_ref):
    o_ref[...] = jnp.dot(x_ref[...], w_ref[...], preferred_element_type=F32).astype(o_ref.dtype)


def _scaled_matmul_kernel(x_ref, w_ref, s_ref, o_ref):
    acc = jnp.dot(x_ref[...], w_ref[...], preferred_element_type=F32)
    o_ref[...] = (acc * s_ref[...]).astype(o_ref.dtype)


def _matmul(x, w, out_dtype, col_scale=None, tm=MM_TM, tn=MM_TN):
    m, k = x.shape
    n = w.shape[1]
    tm, tn = min(tm, m), min(tn, n)
    in_specs = [pl.BlockSpec((tm, k), lambda j, i: (i, 0)), pl.BlockSpec((k, tn), lambda j, i: (0, j))]
    operands = [x, w]
    if col_scale is not None:
        in_specs.append(pl.BlockSpec((1, tn), lambda j, i: (0, j)))
        operands.append(col_scale.reshape(1, n))
    return pl.pallas_call(
        _matmul_kernel if col_scale is None else _scaled_matmul_kernel,
        grid=(n // tn, m // tm),
        in_specs=in_specs,
        out_specs=pl.BlockSpec((tm, tn), lambda j, i: (i, j)),
        out_shape=jax.ShapeDtypeStruct((m, n), out_dtype),
        name="matmul",
        compiler_params=_params("parallel", "parallel"),
    )(*operands)


def _matmul2_kernel(xa_ref, xb_ref, w_ref, o_ref, *, ka):
    acc = jnp.dot(xa_ref[...], w_ref[:ka, :], preferred_element_type=F32)
    acc = acc + jnp.dot(xb_ref[...], w_ref[ka:, :], preferred_element_type=F32)
    o_ref[...] = acc.astype(o_ref.dtype)


def _matmul2(xa, xb, w, out_dtype, tm=MM_TM, tn=MM_TN):
    m, ka = xa.shape
    kb = xb.shape[1]
    n = w.shape[1]
    tm, tn = min(tm, m), min(tn, n)
    return pl.pallas_call(
        functools.partial(_matmul2_kernel, ka=ka),
        grid=(n // tn, m // tm),
        in_specs=[pl.BlockSpec((tm, ka), lambda j, i: (i, 0)),
                  pl.BlockSpec((tm, kb), lambda j, i: (i, 0)),
                  pl.BlockSpec((ka + kb, tn), lambda j, i: (0, j))],
        out_specs=pl.BlockSpec((tm, tn), lambda j, i: (i, j)),
        out_shape=jax.ShapeDtypeStruct((m, n), out_dtype),
        name="matmul2",
        compiler_params=_params("parallel", "parallel"),
    )(xa, xb, w)


def _softmax_step(m_ref, a_ref, st_ref, p_ref, shift, vt):
    tk = st_ref.shape[0]
    strips = range(0, tk, SOFTMAX_STRIP)
    mx = None
    for r in strips:
        s = st_ref[r:r + SOFTMAX_STRIP, :]
        mx = s if mx is None else jnp.maximum(mx, s)
    m_prev = m_ref[...]
    m_next = jnp.maximum(m_prev, jnp.max(mx, axis=0, keepdims=True) - shift)
    alpha = jnp.exp2(m_prev - m_next)
    base = m_next + shift
    for r in strips:
        p_ref[r:r + SOFTMAX_STRIP, :] = jnp.exp2(st_ref[r:r + SOFTMAX_STRIP, :] - base).astype(p_ref.dtype)
    a_ref[...] = alpha * a_ref[...] + jnp.dot(vt, p_ref[...], preferred_element_type=F32)
    m_ref[...] = m_next


def _softmax_init(m_ref, a_ref):
    m_ref[...] = jnp.full(m_ref.shape, MASKED, F32)
    a_ref[...] = jnp.zeros(a_ref.shape, F32)


def _causal_tile_walk(i, scores, softmax):
    @pl.when(i == 0)
    def _():
        scores(i, 0, True)
        softmax(i, 0, True)

    @pl.when(i > 0)
    def _():
        scores(0, 0, False)
        pairs = (i - 1) // 2

        def body(jj, carry):
            a = 2 * jj
            scores(a + 1, 1, False)
            softmax(a, 0, False)
            scores(a + 2, 0, False)
            softmax(a + 1, 1, False)
            return carry

        lax.fori_loop(0, pairs, body, 0)
        a = 2 * pairs

        @pl.when(a == i - 1)
        def _():
            scores(i, 1, True)
            softmax(a, 0, False)
            softmax(i, 1, True)

        @pl.when(a == i - 2)
        def _():
            scores(a + 1, 1, False)
            softmax(a, 0, False)
            scores(i, 0, True)
            softmax(a + 1, 1, False)
            softmax(i, 0, True)


def _softmax_result(a_ref):
    return a_ref[:LANES, :] / a_ref[LANES:LANES + 1, :]


def _cast_specs(weights, batch, nh, nq):
    first = nq // 2
    per_head = nq - first
    steps = batch * nh * per_head

    def block(b, h, i):
        return jnp.maximum((b * nh + h) * per_head + jnp.maximum(i - first, -1), 0), 0

    in_specs, out_specs, out_shapes = [], [], []
    for w in weights:
        rows, cols = w.shape
        blk = rows // steps
        assert blk * steps == rows and blk % BF16_SUBLANES == 0, (w.shape, steps)
        spec = pl.BlockSpec((blk, cols), block)
        in_specs.append(spec)
        out_specs.append(spec)
        out_shapes.append(jax.ShapeDtypeStruct(w.shape, BF16))
    return in_specs, out_specs, out_shapes, first


def _cast_blocks(src_refs, dst_refs, i, first):
    @pl.when(i >= first)
    def _():
        for src, dst in zip(src_refs, dst_refs):
            dst[...] = src[...].astype(dst.dtype)


def _diff_attn_kernel(*refs, t, lam_init, n_cast, cast_from):
    (slopes_ref, lq1_ref, lk1_ref, lq2_ref, lk2_ref, g_ref, q_ref, k_ref, vt_ref), refs = refs[:9], refs[9:]
    cast_src, (o_ref,), cast_dst = refs[:n_cast], refs[n_cast:n_cast + 1], refs[n_cast + 1:2 * n_cast + 1]
    m1, a1, m2, a2, st1a, st1b, st2a, st2b, p1, p2, bias0, bias_diag = refs[2 * n_cast + 1:]
    st1, st2 = (st1a, st1b), (st2a, st2b)
    h = pl.program_id(1)
    i = pl.program_id(2)
    _cast_blocks(cast_src, cast_dst, i, cast_from)
    slope2 = slopes_ref[h] * LOG2E
    lam = (jnp.exp(jnp.sum(lq1_ref[...] * lk1_ref[...], keepdims=True))
           - jnp.exp(jnp.sum(lq2_ref[...] * lk2_ref[...], keepdims=True)) + lam_init)

    q = q_ref[...]
    lane = lax.broadcasted_iota(jnp.int32, q.shape, 1)
    q1 = jnp.where(lane < DIFF_QK_DIM, q, jnp.zeros_like(q))
    q2 = jnp.where(lane >= DIFF_QK_DIM, q, jnp.zeros_like(q))

    rel = lax.broadcasted_iota(jnp.int32, (t, t), 1) - lax.broadcasted_iota(jnp.int32, (t, t), 0)
    bias0[...] = -slope2 * rel.astype(F32)
    bias_diag[...] = jnp.where(rel >= 0, bias0[...], MASKED)

    _softmax_init(m1, a1)
    _softmax_init(m2, a2)

    def scores(j, slot, causal):
        kt = k_ref[pl.ds(pl.multiple_of(j * t, t), t), :]
        bias = bias_diag if causal else bias0
        st1[slot][...] = _nt_dot(kt, q1) + bias[...]
        st2[slot][...] = _nt_dot(kt, q2) + bias[...]

    def softmax(j, slot, causal):
        shift = 0.0 if causal else slope2 * ((i - j) * t).astype(F32)
        vt = vt_ref[j]
        _softmax_step(m1, a1, st1[slot], p1, shift, vt)
        _softmax_step(m2, a2, st2[slot], p2, shift, vt)

    _causal_tile_walk(i, scores, softmax)

    ot = _softmax_result(a1) - lam * _softmax_result(a2)
    ot = ot * lax.rsqrt(jnp.mean(ot * ot, axis=0, keepdims=True) + SUBLN_EPS)
    o_ref[...] = (ot.T * g_ref[...] * (1.0 - lam_init)).astype(o_ref.dtype)


def _tile_transposed_values(v, batch, seq, nh, t):
    v = v.reshape(batch, seq // t, t, nh, LANES).transpose(0, 3, 1, 4, 2)
    v = v.reshape(batch * nh, seq // t, LANES, t)
    ones = jnp.ones(v.shape[:2] + (1, t), v.dtype)
    zeros = jnp.zeros(v.shape[:2] + (DV_ROWS - LANES - 1, t), v.dtype)
    return jnp.concatenate([v, ones, zeros], axis=2)


def _diff_attention(proj, lq1, lk1, lq2, lk2, subln_g, slopes, batch, seq, lam_init, cast_weights):
    t = min(DIFF_T, seq)
    nq = seq // t
    nh = N_DIFF_HEADS
    vt = _tile_transposed_values(proj[:, 2 * nh * LANES:3 * nh * LANES], batch, seq, nh, t)
    vec = lambda n: pl.BlockSpec((1, n), lambda b, h, i: (0, 0))
    c_in, c_out, c_shapes, cast_from = _cast_specs(cast_weights, batch, nh, nq)
    kern = functools.partial(_diff_attn_kernel, t=t, lam_init=lam_init, n_cast=len(cast_weights),
                             cast_from=cast_from)
    stat = pltpu.VMEM((1, t), F32)
    acc = pltpu.VMEM((DV_ROWS, t), F32)
    out = pl.pallas_call(
        kern,
        grid=(batch, nh, nq),
        in_specs=[pl.BlockSpec(memory_space=pltpu.SMEM),
                  vec(DIFF_QK_DIM), vec(DIFF_QK_DIM), vec(DIFF_QK_DIM), vec(DIFF_QK_DIM), vec(LANES),
                  pl.BlockSpec((t, LANES), lambda b, h, i: (b * nq + i, h)),
                  pl.BlockSpec((seq, LANES), lambda b, h, i: (b, nh + h)),
                  pl.BlockSpec((None, nq, DV_ROWS, t), lambda b, h, i: (b * nh + h, 0, 0, 0))] + c_in,
        out_specs=[pl.BlockSpec((t, LANES), lambda b, h, i: (b * nq + i, h))] + c_out,
        out_shape=[jax.ShapeDtypeStruct((batch * seq, nh * LANES), BF16)] + c_shapes,
        scratch_shapes=[stat, acc, stat, acc] + [pltpu.VMEM((t, t), F32)] * 4
                       + [pltpu.VMEM((t, t), BF16)] * 2 + [pltpu.VMEM((t, t), F32)] * 2,
        name="diff_attention",
        compiler_params=_params("arbitrary", "arbitrary", "arbitrary"),
    )(slopes, lq1.reshape(1, -1), lk1.reshape(1, -1), lq2.reshape(1, -1), lk2.reshape(1, -1),
      subln_g.reshape(1, -1), proj, proj, vt, *cast_weights)
    return out[0], out[1:]


def _moba_kernel(*refs, nb, t, n_cast, cast_from):
    (slopes_ref, q_ref, k_ref, vt_ref), refs = refs[:4], refs[4:]
    cast_src, (o_ref,), cast_dst = refs[:n_cast], refs[n_cast:n_cast + 1], refs[n_cast + 1:2 * n_cast + 1]
    km_hi, km_mid, km_lo, selb, m_r, a_r, st_a, st_b, p, bias0, bias_diag = refs[2 * n_cast + 1:]
    st = (st_a, st_b)

    blk = MOBA_BLOCK
    per = t // blk
    h = pl.program_id(1)
    i = pl.program_id(2)
    slope2 = slopes_ref[h] * LOG2E
    _cast_blocks(cast_src, cast_dst, i, cast_from)

    @pl.when(i == 0)
    def _():
        km = jnp.sum(k_ref[...].astype(F32).reshape(nb, blk, LANES), axis=1) * (1.0 / blk)
        hi = km.astype(BF16)
        r1 = km - hi.astype(F32)
        mid = r1.astype(BF16)
        km_hi[...] = hi
        km_mid[...] = mid
        km_lo[...] = (r1 - mid.astype(F32)).astype(BF16)

    q = q_ref[...]
    gate = _nt_dot(km_hi[...], q) + _nt_dot(km_mid[...], q) + _nt_dot(km_lo[...], q)
    blk_id = lax.broadcasted_iota(jnp.int32, gate.shape, 0)
    q_sub = lax.broadcasted_iota(jnp.int32, (1, t), 1) // blk
    past = blk_id < i * per + q_sub
    gate = jnp.where(past, gate, NEG_INF)
    beaten = jnp.zeros(gate.shape, jnp.int32)
    for n in range(nb):
        gn = gate[n:n + 1, :]
        beaten = beaten + jnp.where(gn > gate, 1, jnp.where(gn == gate, jnp.where(blk_id > n, 1, 0), 0))
    selb[...] = jnp.where(past, jnp.where(beaten < MOBA_TOPK, 0.0, MASKED), MASKED)

    rel = lax.broadcasted_iota(jnp.int32, (t, t), 1) - lax.broadcasted_iota(jnp.int32, (t, t), 0)
    bias0[...] = -slope2 * rel.astype(F32)
    bias_diag[...] = jnp.where(rel >= 0, bias0[...], MASKED)
    _softmax_init(m_r, a_r)

    def scores(j, slot, own):
        rows = []
        for u in range(per):
            row = selb[pl.ds(j * per + u, 1), :]
            if own:
                row = jnp.where(q_sub <= u, 0.0, row)
            rows.append(jnp.broadcast_to(row, (blk, t)))
        bias = bias_diag if own else bias0
        st[slot][...] = (_nt_dot(k_ref[pl.ds(pl.multiple_of(j * t, t), t), :], q) + bias[...]
                         + jnp.concatenate(rows, axis=0))

    def softmax(j, slot, own):
        shift = 0.0 if own else slope2 * ((i - j) * t).astype(F32)
        _softmax_step(m_r, a_r, st[slot], p, shift, vt_ref[j])

    _causal_tile_walk(i, scores, softmax)
    o_ref[...] = _softmax_result(a_r).T.astype(o_ref.dtype)


def _moba_attention(proj, slopes, batch, seq, cast_weights):
    blk = MOBA_BLOCK
    t = min(MOBA_T, seq)
    nq = seq // t
    nb = seq // blk
    nh = N_MOBA_HEADS
    base = 3 * N_DIFF_HEADS
    vt = _tile_transposed_values(proj[:, (base + 2 * nh) * LANES:(base + 3 * nh) * LANES], batch, seq, nh, t)
    c_in, c_out, c_shapes, cast_from = _cast_specs(cast_weights, batch, nh, nq)
    kern = functools.partial(_moba_kernel, nb=nb, t=t, n_cast=len(cast_weights), cast_from=cast_from)
    out = pl.pallas_call(
        kern,
        grid=(batch, nh, nq),
        in_specs=[pl.BlockSpec(memory_space=pltpu.SMEM),
                  pl.BlockSpec((t, LANES), lambda b, h, i: (b * nq + i, base + h)),
                  pl.BlockSpec((seq, LANES), lambda b, h, i: (b, base + nh + h)),
                  pl.BlockSpec((None, nq, DV_ROWS, t), lambda b, h, i: (b * nh + h, 0, 0, 0))] + c_in,
        out_specs=[pl.BlockSpec((t, LANES), lambda b, h, i: (b * nq + i, h))] + c_out,
        out_shape=[jax.ShapeDtypeStruct((batch * seq, nh * LANES), BF16)] + c_shapes,
        scratch_shapes=[pltpu.VMEM((nb, LANES), BF16), pltpu.VMEM((nb, LANES), BF16), pltpu.VMEM((nb, LANES), BF16),
                        pltpu.VMEM((nb, t), F32),
                        pltpu.VMEM((1, t), F32), pltpu.VMEM((DV_ROWS, t), F32),
                        pltpu.VMEM((t, t), F32), pltpu.VMEM((t, t), F32), pltpu.VMEM((t, t), BF16),
                        pltpu.VMEM((t, t), F32), pltpu.VMEM((t, t), F32)],
        name="moba_attention",
        compiler_params=_params("arbitrary", "arbitrary", "arbitrary"),
    )(slopes, proj, proj, vt, *cast_weights)
    return out[0], out[1:]


def _swa_kernel(slopes_ref, sinks_ref, q_ref, kp_ref, kc_ref, vp_ref, vc_ref, o_ref, *, group):
    w = SWA_WINDOW
    n = pl.program_id(1)
    g = pl.program_id(2)
    kcat = jnp.concatenate([kp_ref[...], kc_ref[...]], axis=0)
    vcat = jnp.concatenate([vp_ref[...], vc_ref[...]], axis=0)
    r = lax.broadcasted_iota(jnp.int32, (w, 2 * w), 0)
    c = lax.broadcasted_iota(jnp.int32, (w, 2 * w), 1)
    dist = r - c + w
    first_key = jnp.where(n > 0, 0, w)
    ok = jnp.logical_and(jnp.logical_and(dist >= 0, dist < w), c >= first_key)
    distf = dist.astype(F32)
    lane = lax.broadcasted_iota(jnp.int32, (w, LANES), 1)
    low = lane < SWA_HEAD_DIM
    for pair in range(group // 2):
        qp = q_ref[:, pair * LANES:(pair + 1) * LANES]
        outs = []
        for half in range(2):
            head = g * group + pair * 2 + half
            slope2 = slopes_ref[head] * LOG2E
            sink2 = sinks_ref[head] * LOG2E
            qz = jnp.where(low if half == 0 else jnp.logical_not(low), qp, jnp.zeros_like(qp))
            s = _nt_dot(qz, kcat) - slope2 * distf
            s = jnp.where(ok, s, NEG_INF)
            m = jnp.maximum(jnp.max(s, axis=1, keepdims=True), sink2)
            e = jnp.exp2(s - m)
            den = jnp.sum(e, axis=1, keepdims=True) + jnp.exp2(sink2 - m)
            outs.append(jnp.dot(e.astype(BF16), vcat, preferred_element_type=F32) / den)
        o_ref[:, pair * LANES:(pair + 1) * LANES] = jnp.where(low, outs[0], outs[1]).astype(o_ref.dtype)


def _swa_attention(proj, k2, v2, sinks, slopes, batch, seq):
    w = SWA_WINDOW
    nblk = seq // w
    group = N_SWA_HEADS // N_SWA_KV_HEADS
    gw = group * SWA_HEAD_DIM
    prev = lambda b, n, g: (b * nblk + jnp.maximum(n - 1, 0), g)
    cur = lambda b, n, g: (b * nblk + n, g)
    return pl.pallas_call(
        functools.partial(_swa_kernel, group=group),
        grid=(batch, nblk, N_SWA_KV_HEADS),
        in_specs=[pl.BlockSpec(memory_space=pltpu.SMEM), pl.BlockSpec(memory_space=pltpu.SMEM),
                  pl.BlockSpec((w, gw), cur),
                  pl.BlockSpec((w, LANES), prev), pl.BlockSpec((w, LANES), cur),
                  pl.BlockSpec((w, LANES), prev), pl.BlockSpec((w, LANES), cur)],
        out_specs=pl.BlockSpec((w, gw), cur),
        out_shape=jax.ShapeDtypeStruct((batch * seq, N_SWA_HEADS * SWA_HEAD_DIM), BF16),
        name="swa_attention",
        compiler_params=_params("parallel", "parallel", "parallel"),
    )(slopes, sinks, proj, k2, k2, v2, v2)


def _layer_norm(z, g, b):
    mu = jnp.mean(z, axis=-1, keepdims=True)
    zc = z - mu
    var = jnp.mean(zc * zc, axis=-1, keepdims=True)
    return zc * lax.rsqrt(var + LN_EPS) * g + b


def _split3(a):
    hi = a.astype(BF16)
    r1 = a - hi.astype(F32)
    mid = r1.astype(BF16)
    return hi, mid, (r1 - mid.astype(F32)).astype(BF16)


def _norm_router_kernel(x_ref, h_ref, g_ref, b_ref, rw_ref, rb_ref,
                        y_ref, eid_ref, w_ref, pos_ref, cnt_ref, carry, *, tm):
    step = pl.program_id(0)

    @pl.when(step == 0)
    def _():
        carry[...] = jnp.zeros(carry.shape, F32)

    y = _layer_norm(DEEPNORM_ALPHA * x_ref[...] + h_ref[...], g_ref[...], b_ref[...])
    y_ref[...] = y

    yh, ym, yl = _split3(y)
    wh, wm, wl = _split3(rw_ref[...])
    dot = lambda a, b: jnp.dot(a, b, preferred_element_type=F32)
    logits = (dot(yl, wh) + dot(ym, wm) + dot(yh, wl)) + (dot(ym, wh) + dot(yh, wm)) + dot(yh, wh) + rb_ref[...]

    e_id = lax.broadcasted_iota(jnp.int32, logits.shape, 1)
    beaten = jnp.zeros(logits.shape, jnp.int32)
    for n in range(N_EXPERTS):
        ln = logits[:, n:n + 1]
        beaten = beaten + jnp.where(ln > logits, 1, jnp.where(ln == logits, jnp.where(e_id > n, 1, 0), 0))
    sel = beaten < TOP_K
    ex = jnp.where(sel, jnp.exp(logits - jnp.max(logits, axis=1, keepdims=True)), 0.0)
    weight = ex / jnp.sum(ex, axis=1, keepdims=True)

    self = jnp.where(sel, 1.0, 0.0)
    rr = lax.broadcasted_iota(jnp.int32, (tm, tm), 0)
    cc = lax.broadcasted_iota(jnp.int32, (tm, tm), 1)
    before = jnp.where(rr > cc, 1.0, 0.0).astype(BF16)
    pos = carry[...] + jnp.dot(before, self.astype(BF16), preferred_element_type=F32)
    carry[...] = carry[...] + jnp.sum(self, axis=0, keepdims=True)
    cnt_ref[...] = carry[...].astype(jnp.int32)

    e_f = e_id.astype(F32)
    k_id = lax.broadcasted_iota(jnp.int32, (tm, TOP_K), 1)
    e_k = jnp.zeros((tm, TOP_K), F32)
    w_k = jnp.zeros((tm, TOP_K), F32)
    p_k = jnp.zeros((tm, TOP_K), F32)
    for k in range(TOP_K):
        hit = beaten == k
        pick = lambda a: jnp.sum(jnp.where(hit, a, 0.0), axis=1, keepdims=True)
        e_k = jnp.where(k_id == k, pick(e_f), e_k)
        w_k = jnp.where(k_id == k, pick(weight), w_k)
        p_k = jnp.where(k_id == k, pick(pos), p_k)
    eid_ref[...] = e_k.astype(jnp.int32)
    w_ref[...] = w_k
    pos_ref[...] = p_k.astype(jnp.int32)


def _norm_router(x, h, g, b, rw, rb):
    t, d = x.shape
    tm = min(NORM_TM, t)
    ne = rw.shape[1]
    row = pl.BlockSpec((tm, d), lambda i: (i, 0))
    vec = pl.BlockSpec((1, d), lambda i: (0, 0))
    per_k = pl.BlockSpec((tm, TOP_K), lambda i: (i, 0))
    return pl.pallas_call(
        functools.partial(_norm_router_kernel, tm=tm),
        grid=(t // tm,),
        in_specs=[row, row, vec, vec, pl.BlockSpec((d, ne), lambda i: (0, 0)), pl.BlockSpec((1, ne), lambda i: (0, 0))],
        out_specs=[row, per_k, per_k, per_k, pl.BlockSpec((1, ne), lambda i: (0, 0))],
        out_shape=[jax.ShapeDtypeStruct((t, d), F32), jax.ShapeDtypeStruct((t, TOP_K), jnp.int32),
                   jax.ShapeDtypeStruct((t, TOP_K), F32), jax.ShapeDtypeStruct((t, TOP_K), jnp.int32),
                   jax.ShapeDtypeStruct((1, ne), jnp.int32)],
        scratch_shapes=[pltpu.VMEM((1, ne), F32)],
        name="norm_router",
        compiler_params=_params("arbitrary"),
    )(x, h, g.reshape(1, d), b.reshape(1, d), rw, rb.reshape(1, ne))


def _row_copy(src_ref, dst_ref, src_row, dst_row, sem):
    return pltpu.make_async_copy(src_ref.at[pl.ds(src_row, 1), :], dst_ref.at[pl.ds(dst_row, 1), :], sem)


def _dispatch_kernel(slot_ref, pad_lo_ref, pad_hi_ref, nu_ref, y_ref, xs_hbm, zrow, sem, pad_sem, *, tm, ne):
    step = pl.program_id(0)

    @pl.when(step == 0)
    def _():
        zrow[...] = jnp.zeros(zrow.shape, zrow.dtype)
        tile_rows = zrow.shape[0]
        tail = lambda tl: pltpu.make_async_copy(
            zrow, xs_hbm.at[pl.ds(pl.multiple_of(tl * tile_rows, tile_rows), tile_rows), :], pad_sem)

        def fill_tile(tl, carry):
            tail(tl).start()
            return carry

        def tile_filled(tl, carry):
            tail(tl).wait()
            return carry

        lax.fori_loop(nu_ref[0], xs_hbm.shape[0] // tile_rows, fill_tile, 0)
        lax.fori_loop(nu_ref[0], xs_hbm.shape[0] // tile_rows, tile_filled, 0)
        for e in range(ne):
            lo = pad_lo_ref[e]
            hi = pad_hi_ref[e]

            def fill(r, carry):
                _row_copy(zrow, xs_hbm, 0, r, pad_sem).start()
                return carry

            def filled(r, carry):
                _row_copy(zrow, xs_hbm, 0, r, pad_sem).wait()
                return carry

            lax.fori_loop(lo, hi, fill, 0)
            lax.fori_loop(lo, hi, filled, 0)

    base = step * tm * TOP_K

    def issue(r, carry):
        for k in range(TOP_K):
            _row_copy(y_ref, xs_hbm, r, slot_ref[base + r * TOP_K + k], sem).start()
        return carry

    def drain(r, carry):
        for k in range(TOP_K):
            _row_copy(y_ref, xs_hbm, r, slot_ref[base + r * TOP_K + k], sem).wait()
        return carry

    lax.fori_loop(0, tm, issue, 0, unroll=4)
    lax.fori_loop(0, tm, drain, 0, unroll=4)


def _dispatch_rows(y, slots, pad_lo, pad_hi, n_used, n_rows):
    t, d = y.shape
    tm = min(DISPATCH_T, t)
    ne = pad_lo.shape[0]
    return pl.pallas_call(
        functools.partial(_dispatch_kernel, tm=tm, ne=ne),
        grid_spec=pltpu.PrefetchScalarGridSpec(
            num_scalar_prefetch=4,
            grid=(t // tm,),
            in_specs=[pl.BlockSpec((tm, d), lambda i, s, lo, hi, nu: (i, 0))],
            out_specs=pl.BlockSpec(memory_space=pl.ANY),
            scratch_shapes=[pltpu.VMEM((MOE_TM, d), y.dtype), pltpu.SemaphoreType.DMA(()),
                            pltpu.SemaphoreType.DMA(())]),
        out_shape=jax.ShapeDtypeStruct((n_rows, d), y.dtype),
        name="dispatch_scatter",
        compiler_params=_params("arbitrary"),
    )(slots, pad_lo, pad_hi, n_used, y)


def _gate_up_kernel(te_ref, nu_ref, x_ref, wg_ref, wu_ref, bg_ref, bu_ref, h_ref):
    t = pl.program_id(0)

    @pl.when(t < nu_ref[0])
    def _():
        x = x_ref[...].astype(BF16)
        gate = jnp.minimum(jnp.dot(x, wg_ref[0], preferred_element_type=F32) + bg_ref[0], SWIGLU_LIMIT)
        up = jnp.clip(jnp.dot(x, wu_ref[0], preferred_element_type=F32) + bu_ref[0], -SWIGLU_LIMIT, SWIGLU_LIMIT)
        h_ref[...] = ((up + 1.0) * gate * jax.nn.sigmoid(SWIGLU_ALPHA * gate)).astype(h_ref.dtype)

    @pl.when(t >= nu_ref[0])
    def _():
        h_ref[...] = jnp.zeros(h_ref.shape, h_ref.dtype)


def _down_kernel(te_ref, nu_ref, h_ref, wd_ref, bd_ref, y_ref, wd_bf):
    t = pl.program_id(0)

    @pl.when(jnp.logical_or(t == 0, te_ref[t] != te_ref[jnp.maximum(t - 1, 0)]))
    def _():
        wd_bf[...] = wd_ref[0].astype(BF16)

    @pl.when(t < nu_ref[0])
    def _():
        y_ref[...] = jnp.dot(h_ref[...], wd_bf[...], preferred_element_type=F32) + bd_ref[0]

    @pl.when(t >= nu_ref[0])
    def _():
        y_ref[...] = jnp.zeros(y_ref.shape, y_ref.dtype)


def _expert_ffn(xs, tile_expert, n_used, w_gate, b_gate, w_up, b_up, w_down, b_down):
    p, d = xs.shape
    ne, _, f = w_gate.shape
    tm = MOE_TM
    nt = p // tm
    used = lambda t, nu: jnp.minimum(t, nu[0] - 1)
    expert = lambda t, te, nu: (te[t], 0, 0)
    h = pl.pallas_call(
        _gate_up_kernel,
        grid_spec=pltpu.PrefetchScalarGridSpec(
            num_scalar_prefetch=2,
            grid=(nt,),
            in_specs=[pl.BlockSpec((tm, d), lambda t, te, nu: (used(t, nu), 0)),
                      pl.BlockSpec((1, d, f), expert), pl.BlockSpec((1, d, f), expert),
                      pl.BlockSpec((1, 1, f), expert), pl.BlockSpec((1, 1, f), expert)],
            out_specs=pl.BlockSpec((tm, f), lambda t, te, nu: (t, 0))),
        out_shape=jax.ShapeDtypeStruct((p, f), BF16),
        name="expert_gate_up",
        compiler_params=_params("arbitrary"),
    )(tile_expert, n_used, xs, w_gate, w_up, b_gate.reshape(ne, 1, f), b_up.reshape(ne, 1, f))
    return pl.pallas_call(
        _down_kernel,
        grid_spec=pltpu.PrefetchScalarGridSpec(
            num_scalar_prefetch=2,
            grid=(nt,),
            in_specs=[pl.BlockSpec((tm, f), lambda t, te, nu: (used(t, nu), 0)),
                      pl.BlockSpec((1, f, d), expert), pl.BlockSpec((1, 1, d), expert)],
            out_specs=pl.BlockSpec((tm, d), lambda t, te, nu: (t, 0)),
            scratch_shapes=[pltpu.VMEM((f, d), BF16)]),
        out_shape=jax.ShapeDtypeStruct((p, d), F32),
        name="expert_down",
        compiler_params=_params("arbitrary"),
    )(tile_expert, n_used, h, w_down, b_down.reshape(ne, 1, d))


def _combine_norm_kernel(slot_ref, x_ref, w_ref, g_ref, b_ref, ys_hbm, o_ref, obf_ref, buf, sems, *, tc):
    step = pl.program_id(0)
    cur = step % 2

    def fetch(s, half):
        base = s * tc * TOP_K

        def issue(r, carry):
            for k in range(TOP_K):
                _row_copy(ys_hbm, buf.at[half, k], slot_ref[base + r * TOP_K + k], r, sems.at[half]).start()
            return carry

        lax.fori_loop(0, tc, issue, 0, unroll=4)

    @pl.when(step == 0)
    def _():
        fetch(0, 0)

    @pl.when(step + 1 < pl.num_programs(0))
    def _():
        fetch(step + 1, 1 - cur)

    def drain(r, carry):
        for k in range(TOP_K):
            _row_copy(ys_hbm, buf.at[cur, k], 0, r, sems.at[cur]).wait()
        return carry

    lax.fori_loop(0, tc, drain, 0, unroll=4)

    w = w_ref[...]
    moe = w[:, 0:1] * buf[cur, 0]
    for k in range(1, TOP_K):
        moe = moe + w[:, k:k + 1] * buf[cur, k]
    out = _layer_norm(DEEPNORM_ALPHA * x_ref[...] + moe, g_ref[...], b_ref[...])
    o_ref[...] = out
    obf_ref[...] = out.astype(obf_ref.dtype)


def _combine_norm(x, ys, slots, w4, g, b):
    t, d = x.shape
    tc = min(COMBINE_T, t)
    row = pl.BlockSpec((tc, d), lambda i, s: (i, 0))
    return pl.pallas_call(
        functools.partial(_combine_norm_kernel, tc=tc),
        grid_spec=pltpu.PrefetchScalarGridSpec(
            num_scalar_prefetch=1,
            grid=(t // tc,),
            in_specs=[row,
                      pl.BlockSpec((tc, TOP_K), lambda i, s: (i, 0)),
                      pl.BlockSpec((1, d), lambda i, s: (0, 0)),
                      pl.BlockSpec((1, d), lambda i, s: (0, 0)),
                      pl.BlockSpec(memory_space=pl.ANY)],
            out_specs=[row, row],
            scratch_shapes=[pltpu.VMEM((2, TOP_K, tc, d), F32), pltpu.SemaphoreType.DMA((2,))]),
        out_shape=[jax.ShapeDtypeStruct((t, d), F32), jax.ShapeDtypeStruct((t, d), BF16)],
        name="combine_norm",
        compiler_params=_params("arbitrary"),
    )(slots, x, w4, g.reshape(1, d), b.reshape(1, d), ys)


def _routing_tables(eid, pos, counts, tm):
    t = eid.shape[0]
    counts = counts[0]
    ne = counts.shape[0]
    padded = ((counts + tm - 1) // tm) * tm
    ends = jnp.cumsum(padded)
    starts = ends - padded
    n_tiles = (t * TOP_K) // tm + ne
    n_used = (ends[-1] // tm).astype(jnp.int32)
    tile_start = jnp.arange(n_tiles, dtype=jnp.int32) * tm
    tile_expert = jnp.sum((tile_start[:, None] >= ends[None, :]).astype(jnp.int32), axis=1)
    tile_expert = jnp.minimum(tile_expert, ne - 1)
    last = jnp.sum(jnp.where(jnp.arange(n_tiles) == n_used - 1, tile_expert, 0))
    tile_expert = jnp.where(jnp.arange(n_tiles) < n_used, tile_expert, last).astype(jnp.int32)
    group_start = jnp.sum(jnp.where(eid[:, :, None] == jnp.arange(ne)[None, None, :], starts[None, None, :], 0), axis=2)
    slots = (group_start + pos).astype(jnp.int32).reshape(-1)
    return slots, tile_expert, n_used.reshape(1), (starts + counts).astype(jnp.int32), ends.astype(jnp.int32), n_tiles * tm


def _post_norm_moe(x, h, ln1_g, ln1_b, router_w, router_b, w_gate, b_gate, w_up, b_up, w_down, b_down,
                   ln2_g, ln2_b):
    y, eid, w4, pos, counts = _norm_router(x, h, ln1_g, ln1_b, router_w, router_b)
    slots, tile_expert, n_used, pad_lo, pad_hi, n_rows = _routing_tables(eid, pos, counts, MOE_TM)
    xs = _dispatch_rows(y, slots, pad_lo, pad_hi, n_used, n_rows)
    ys = _expert_ffn(xs, tile_expert, n_used, w_gate, b_gate, w_up, b_up, w_down, b_down)
    return _combine_norm(y, ys, slots, w4, ln2_g, ln2_b)


def kernel(x, l0_w_in, l0_lambda_q1, l0_lambda_k1, l0_lambda_q2, l0_lambda_k2, l0_subln_g, l0_w_o, l0_ln1_g, l0_ln1_b, l0_router_w, l0_router_b, l0_w_gate, l0_b_gate, l0_w_up, l0_b_up, l0_w_down, l0_b_down, l0_ln2_g, l0_ln2_b, l1_w_in, l1_sinks, l1_w_o, l1_ln1_g, l1_ln1_b, l1_router_w, l1_router_b, l1_w_gate, l1_b_gate, l1_w_up, l1_b_up, l1_w_down, l1_b_down, l1_ln2_g, l1_ln2_b):
    batch, seq, d = x.shape
    xt = x.reshape(batch * seq, d)
    flat = lambda w: w.reshape(-1, w.shape[-1])
    like = lambda wb, w: wb.reshape(w.shape)

    a_w = N_DIFF_HEADS * 2 * DIFF_QK_DIM
    b_w = N_MOBA_HEADS * LANES
    one = lambda n: jnp.ones((n,), F32)
    scale0 = jnp.concatenate([one(a_w) * (LOG2E * DIFF_QK_DIM ** -0.5), one(2 * a_w),
                              one(b_w) * (LOG2E * LANES ** -0.5), one(2 * b_w)])
    proj = _matmul(xt.astype(BF16), l0_w_in.astype(BF16), BF16, col_scale=scale0)
    slopes = _alibi_slopes(N_DIFF_HEADS + N_MOBA_HEADS)
    lam_init = 0.8 - 0.6 * math.exp(-0.3 * 0)
    attn_a, (wg0, wu0, wo0) = _diff_attention(
        proj, l0_lambda_q1, l0_lambda_k1, l0_lambda_q2, l0_lambda_k2, l0_subln_g, slopes[0::2], batch, seq,
        lam_init, [flat(l0_w_gate), flat(l0_w_up), l0_w_o])
    attn_b, (wg1, wu1, wi1, wo1) = _moba_attention(
        proj, slopes[1::2], batch, seq, [flat(l1_w_gate), flat(l1_w_up), l1_w_in, l1_w_o])
    h = _matmul2(attn_a, attn_b, wo0, F32)
    xt, xt_bf = _post_norm_moe(xt, h, l0_ln1_g, l0_ln1_b, l0_router_w, l0_router_b, like(wg0, l0_w_gate), l0_b_gate,
                               like(wu0, l0_w_up), l0_b_up, l0_w_down, l0_b_down, l0_ln2_g, l0_ln2_b)

    q_w = N_SWA_HEADS * SWA_HEAD_DIM
    kv_w = N_SWA_KV_HEADS * SWA_HEAD_DIM
    scale1 = jnp.concatenate([one(q_w) * (LOG2E * SWA_HEAD_DIM ** -0.5), one(2 * kv_w)])
    proj = _matmul(xt_bf, wi1, BF16, col_scale=scale1)
    dup = lambda a: jnp.broadcast_to(a.reshape(-1, N_SWA_KV_HEADS, 1, SWA_HEAD_DIM),
                                     (a.shape[0], N_SWA_KV_HEADS, 2, SWA_HEAD_DIM)).reshape(a.shape[0], 2 * kv_w)
    k2 = dup(proj[:, q_w:q_w + kv_w])
    v2 = dup(proj[:, q_w + kv_w:])
    attn = _swa_attention(proj, k2, v2, l1_sinks, _alibi_slopes(N_SWA_HEADS), batch, seq)
    h = _matmul(attn, wo1, F32)
    xt, _ = _post_norm_moe(xt, h, l1_ln1_g, l1_ln1_b, l1_router_w, l1_router_b, like(wg1, l1_w_gate), l1_b_gate,
                           like(wu1, l1_w_up), l1_b_up, l1_w_down, l1_b_down, l1_ln2_g, l1_ln2_b)
    return xt.reshape(batch, seq, d)
```

```python
import functools
import math

import jax
import jax.numpy as jnp
from jax import lax
from jax.experimental import pallas as pl
from jax.experimental.pallas import tpu as pltpu

F32 = jnp.float32
BF16 = jnp.bfloat16
NEG_INF = float("-inf")
MASKED = -1e30

DEPTH = 2
DIFF_QK_DIM = 64
N_DIFF_HEADS = 16
N_MOBA_HEADS = 16
MOBA_BLOCK = 256
MOBA_TOPK = 3
N_SWA_HEADS = 64
N_SWA_KV_HEADS = 8
SWA_HEAD_DIM = 64
SWA_WINDOW = 128
N_EXPERTS = 32
TOP_K = 4
SWIGLU_LIMIT = 7.0
SWIGLU_ALPHA = 1.702
LN_EPS = 1e-5
SUBLN_EPS = 1e-5
DEEPNORM_ALPHA = (2.0 * DEPTH) ** 0.25
LOG2E = math.log2(math.e)

LANES = 128
BF16_SUBLANES = 16
DV_ROWS = LANES + BF16_SUBLANES
VMEM_LIMIT = 56 * 1024 * 1024

MM_TM, MM_TN = 512, 1024
DIFF_T = 512
MOBA_T = 512
SOFTMAX_STRIP = 32
SWA_KV_PER_STEP = 4
NORM_TM = 256
MOE_TM = 256
DISPATCH_T = 128
COMBINE_T = 128


def _params(*sem):
    return pltpu.CompilerParams(dimension_semantics=sem, vmem_limit_bytes=VMEM_LIMIT)


def _alibi_slopes(n):
    return 2.0 ** (-8.0 * (jnp.arange(n, dtype=F32) + 1.0) / n)


def _nt_dot(a, b):
    return lax.dot_general(a, b, (((1,), (1,)), ((), ())), preferred_element_type=F32)


def _matmul_kernel(x_ref, w_ref, o_ref):
    o_ref[...] = jnp.dot(x_ref[...], w_ref[...], preferred_element_type=F32).astype(o_ref.dtype)


def _scaled_matmul_kernel(x_ref, w_ref, s_ref, o_ref, wb_ref):
    @pl.when(pl.program_id(1) == 0)
    def _():
        wb_ref[...] = w_ref[...].astype(wb_ref.dtype)

    acc = jnp.dot(x_ref[...], wb_ref[...], preferred_element_type=F32)
    o_ref[...] = (acc * s_ref[...]).astype(o_ref.dtype)


def _matmul(x, w, out_dtype, col_scale=None, tm=MM_TM, tn=MM_TN):
    m, k = x.shape
    n = w.shape[1]
    tm, tn = min(tm, m), min(tn, n)
    in_specs = [pl.BlockSpec((tm, k), lambda j, i: (i, 0)), pl.BlockSpec((k, tn), lambda j, i: (0, j))]
    operands = [x, w]
    scratch = []
    if col_scale is not None:
        in_specs.append(pl.BlockSpec((1, tn), lambda j, i: (0, j)))
        operands.append(col_scale.reshape(1, n))
        scratch.append(pltpu.VMEM((k, tn), BF16))
    return pl.pallas_call(
        _matmul_kernel if col_scale is None else _scaled_matmul_kernel,
        grid=(n // tn, m // tm),
        in_specs=in_specs,
        out_specs=pl.BlockSpec((tm, tn), lambda j, i: (i, j)),
        out_shape=jax.ShapeDtypeStruct((m, n), out_dtype),
        scratch_shapes=scratch,
        name="matmul",
        compiler_params=_params("arbitrary", "arbitrary"),
    )(*operands)


def _matmul2_kernel(xa_ref, xb_ref, w_ref, o_ref, *, ka):
    acc = jnp.dot(xa_ref[...], w_ref[:ka, :], preferred_element_type=F32)
    acc = acc + jnp.dot(xb_ref[...], w_ref[ka:, :], preferred_element_type=F32)
    o_ref[...] = acc.astype(o_ref.dtype)


def _matmul2(xa, xb, w, out_dtype, tm=MM_TM, tn=MM_TN):
    m, ka = xa.shape
    kb = xb.shape[1]
    n = w.shape[1]
    tm, tn = min(tm, m), min(tn, n)
    return pl.pallas_call(
        functools.partial(_matmul2_kernel, ka=ka),
        grid=(n // tn, m // tm),
        in_specs=[pl.BlockSpec((tm, ka), lambda j, i: (i, 0)),
                  pl.BlockSpec((tm, kb), lambda j, i: (i, 0)),
                  pl.BlockSpec((ka + kb, tn), lambda j, i: (0, j))],
        out_specs=pl.BlockSpec((tm, tn), lambda j, i: (i, j)),
        out_shape=jax.ShapeDtypeStruct((m, n), out_dtype),
        name="matmul2",
        compiler_params=_params("parallel", "parallel"),
    )(xa, xb, w)


def _softmax_step(m_ref, a_ref, st_ref, p_ref, shift, vt):
    tk = st_ref.shape[0]
    strips = range(0, tk, SOFTMAX_STRIP)
    mx = None
    for r in strips:
        s = st_ref[r:r + SOFTMAX_STRIP, :]
        mx = s if mx is None else jnp.maximum(mx, s)
    m_prev = m_ref[...]
    m_next = jnp.maximum(m_prev, jnp.max(mx, axis=0, keepdims=True) - shift)
    alpha = jnp.exp2(m_prev - m_next)
    base = m_next + shift
    for r in strips:
        p_ref[r:r + SOFTMAX_STRIP, :] = jnp.exp2(st_ref[r:r + SOFTMAX_STRIP, :] - base).astype(p_ref.dtype)
    a_ref[...] = alpha * a_ref[...] + jnp.dot(vt, p_ref[...], preferred_element_type=F32)
    m_ref[...] = m_next


def _softmax_init(m_ref, a_ref):
    m_ref[...] = jnp.full(m_ref.shape, MASKED, F32)
    a_ref[...] = jnp.zeros(a_ref.shape, F32)


def _causal_tile_walk(i, scores, softmax):
    @pl.when(i == 0)
    def _():
        scores(i, 0, True)
        softmax(i, 0, True)

    @pl.when(i > 0)
    def _():
        scores(0, 0, False)
        pairs = (i - 1) // 2

        def body(jj, carry):
            a = 2 * jj
            scores(a + 1, 1, False)
            softmax(a, 0, False)
            scores(a + 2, 0, False)
            softmax(a + 1, 1, False)
            return carry

        lax.fori_loop(0, pairs, body, 0)
        a = 2 * pairs

        @pl.when(a == i - 1)
        def _():
            scores(i, 1, True)
            softmax(a, 0, False)
            softmax(i, 1, True)

        @pl.when(a == i - 2)
        def _():
            scores(a + 1, 1, False)
            softmax(a, 0, False)
            scores(i, 0, True)
            softmax(a + 1, 1, False)
            softmax(i, 0, True)


def _alibi_tiles(bias0, bias_diag, slope2, i):
    @pl.when(i == 0)
    def _():
        t = bias0.shape[0]
        rel = lax.broadcasted_iota(jnp.int32, (t, t), 1) - lax.broadcasted_iota(jnp.int32, (t, t), 0)
        bias = -slope2 * rel.astype(F32)
        bias0[...] = bias
        bias_diag[...] = jnp.where(rel >= 0, bias, MASKED)


def _softmax_result(a_ref):
    return a_ref[:LANES, :] / a_ref[LANES:LANES + 1, :]


def _cast_specs(weights, batch, nh, nq):
    first = nq // 2
    per_head = nq - first
    steps = batch * nh * per_head

    def block(b, h, i):
        return jnp.maximum((b * nh + h) * per_head + jnp.maximum(i - first, -1), 0), 0

    in_specs, out_specs, out_shapes = [], [], []
    for w in weights:
        rows, cols = w.shape
        blk = rows // steps
        assert blk * steps == rows and blk % BF16_SUBLANES == 0, (w.shape, steps)
        spec = pl.BlockSpec((blk, cols), block)
        in_specs.append(spec)
        out_specs.append(spec)
        out_shapes.append(jax.ShapeDtypeStruct(w.shape, BF16))
    return in_specs, out_specs, out_shapes, first


def _cast_blocks(src_refs, dst_refs, i, first):
    @pl.when(i >= first)
    def _():
        for src, dst in zip(src_refs, dst_refs):
            dst[...] = src[...].astype(dst.dtype)


def _diff_attn_kernel(*refs, t, lam_init, n_cast, cast_from):
    (slopes_ref, lq1_ref, lk1_ref, lq2_ref, lk2_ref, g_ref, q_ref, k_ref, vt_ref), refs = refs[:9], refs[9:]
    cast_src, (o_ref,), cast_dst = refs[:n_cast], refs[n_cast:n_cast + 1], refs[n_cast + 1:2 * n_cast + 1]
    m1, a1, m2, a2, st1a, st1b, st2a, st2b, p1, p2, bias0, bias_diag = refs[2 * n_cast + 1:]
    st1, st2 = (st1a, st1b), (st2a, st2b)
    h = pl.program_id(1)
    i = pl.program_id(2)
    _cast_blocks(cast_src, cast_dst, i, cast_from)
    slope2 = slopes_ref[h] * LOG2E
    lam = (jnp.exp(jnp.sum(lq1_ref[...] * lk1_ref[...], keepdims=True))
           - jnp.exp(jnp.sum(lq2_ref[...] * lk2_ref[...], keepdims=True)) + lam_init)

    q = q_ref[...]
    lane = lax.broadcasted_iota(jnp.int32, q.shape, 1)
    q1 = jnp.where(lane < DIFF_QK_DIM, q, jnp.zeros_like(q))
    q2 = jnp.where(lane >= DIFF_QK_DIM, q, jnp.zeros_like(q))

    _alibi_tiles(bias0, bias_diag, slope2, i)

    _softmax_init(m1, a1)
    _softmax_init(m2, a2)

    def scores(j, slot, causal):
        kt = k_ref[pl.ds(pl.multiple_of(j * t, t), t), :]
        bias = bias_diag if causal else bias0
        st1[slot][...] = _nt_dot(kt, q1) + bias[...]
        st2[slot][...] = _nt_dot(kt, q2) + bias[...]

    def softmax(j, slot, causal):
        shift = 0.0 if causal else slope2 * ((i - j) * t).astype(F32)
        vt = vt_ref[j]
        _softmax_step(m1, a1, st1[slot], p1, shift, vt)
        _softmax_step(m2, a2, st2[slot], p2, shift, vt)

    _causal_tile_walk(i, scores, softmax)

    ot = _softmax_result(a1) - lam * _softmax_result(a2)
    ot = ot * lax.rsqrt(jnp.mean(ot * ot, axis=0, keepdims=True) + SUBLN_EPS)
    o_ref[...] = (ot.T * g_ref[...] * (1.0 - lam_init)).astype(o_ref.dtype)


def _tile_transposed_values(v, batch, seq, nh, t):
    v = v.reshape(batch, seq // t, t, nh, LANES).transpose(0, 3, 1, 4, 2)
    v = v.reshape(batch * nh, seq // t, LANES, t)
    ones = jnp.ones(v.shape[:2] + (1, t), v.dtype)
    zeros = jnp.zeros(v.shape[:2] + (DV_ROWS - LANES - 1, t), v.dtype)
    return jnp.concatenate([v, ones, zeros], axis=2)


def _diff_attention(proj, lq1, lk1, lq2, lk2, subln_g, slopes, batch, seq, lam_init, cast_weights):
    t = min(DIFF_T, seq)
    nq = seq // t
    nh = N_DIFF_HEADS
    vt = _tile_transposed_values(proj[:, 2 * nh * LANES:3 * nh * LANES], batch, seq, nh, t)
    vec = lambda n: pl.BlockSpec((1, n), lambda b, h, i: (0, 0))
    c_in, c_out, c_shapes, cast_from = _cast_specs(cast_weights, batch, nh, nq)
    kern = functools.partial(_diff_attn_kernel, t=t, lam_init=lam_init, n_cast=len(cast_weights),
                             cast_from=cast_from)
    stat = pltpu.VMEM((1, t), F32)
    acc = pltpu.VMEM((DV_ROWS, t), F32)
    out = pl.pallas_call(
        kern,
        grid=(batch, nh, nq),
        in_specs=[pl.BlockSpec(memory_space=pltpu.SMEM),
                  vec(DIFF_QK_DIM), vec(DIFF_QK_DIM), vec(DIFF_QK_DIM), vec(DIFF_QK_DIM), vec(LANES),
                  pl.BlockSpec((t, LANES), lambda b, h, i: (b * nq + i, h)),
                  pl.BlockSpec((seq, LANES), lambda b, h, i: (b, nh + h)),
                  pl.BlockSpec((None, nq, DV_ROWS, t), lambda b, h, i: (b * nh + h, 0, 0, 0))] + c_in,
        out_specs=[pl.BlockSpec((t, LANES), lambda b, h, i: (b * nq + i, h))] + c_out,
        out_shape=[jax.ShapeDtypeStruct((batch * seq, nh * LANES), BF16)] + c_shapes,
        scratch_shapes=[stat, acc, stat, acc] + [pltpu.VMEM((t, t), F32)] * 4
                       + [pltpu.VMEM((t, t), BF16)] * 2 + [pltpu.VMEM((t, t), F32)] * 2,
        name="diff_attention",
        compiler_params=_params("arbitrary", "arbitrary", "arbitrary"),
    )(slopes, lq1.reshape(1, -1), lk1.reshape(1, -1), lq2.reshape(1, -1), lk2.reshape(1, -1),
      subln_g.reshape(1, -1), proj, proj, vt, *cast_weights)
    return out[0], out[1:]


def _moba_kernel(*refs, nb, t, n_cast, cast_from):
    (slopes_ref, q_ref, k_ref, vt_ref), refs = refs[:4], refs[4:]
    cast_src, (o_ref,), cast_dst = refs[:n_cast], refs[n_cast:n_cast + 1], refs[n_cast + 1:2 * n_cast + 1]
    km_hi, km_mid, km_lo, selb, m_r, a_r, st_a, st_b, p, bias0, bias_diag = refs[2 * n_cast + 1:]
    st = (st_a, st_b)

    blk = MOBA_BLOCK
    per = t // blk
    h = pl.program_id(1)
    i = pl.program_id(2)
    slope2 = slopes_ref[h] * LOG2E
    _cast_blocks(cast_src, cast_dst, i, cast_from)

    @pl.when(i == 0)
    def _():
        km = jnp.sum(k_ref[...].astype(F32).reshape(nb, blk, LANES), axis=1) * (1.0 / blk)
        hi = km.astype(BF16)
        r1 = km - hi.astype(F32)
        mid = r1.astype(BF16)
        km_hi[...] = hi
        km_mid[...] = mid
        km_lo[...] = (r1 - mid.astype(F32)).astype(BF16)

    q = q_ref[...]
    gate = _nt_dot(km_hi[...], q) + _nt_dot(km_mid[...], q) + _nt_dot(km_lo[...], q)
    blk_id = lax.broadcasted_iota(jnp.int32, gate.shape, 0)
    q_sub = lax.broadcasted_iota(jnp.int32, (1, t), 1) // blk
    past = blk_id < i * per + q_sub
    gate = jnp.where(past, gate, NEG_INF)
    beaten = jnp.zeros(gate.shape, jnp.int32)
    for n in range(nb):
        gn = gate[n:n + 1, :]
        beaten = beaten + jnp.where(gn > gate, 1, jnp.where(gn == gate, jnp.where(blk_id > n, 1, 0), 0))
    selb[...] = jnp.where(past, jnp.where(beaten < MOBA_TOPK, 0.0, MASKED), MASKED)

    _alibi_tiles(bias0, bias_diag, slope2, i)
    _softmax_init(m_r, a_r)

    def scores(j, slot, own):
        rows = []
        for u in range(per):
            row = selb[pl.ds(j * per + u, 1), :]
            if own:
                row = jnp.where(q_sub <= u, 0.0, row)
            rows.append(jnp.broadcast_to(row, (blk, t)))
        bias = bias_diag if own else bias0
        st[slot][...] = (_nt_dot(k_ref[pl.ds(pl.multiple_of(j * t, t), t), :], q) + bias[...]
                         + jnp.concatenate(rows, axis=0))

    def softmax(j, slot, own):
        shift = 0.0 if own else slope2 * ((i - j) * t).astype(F32)
        _softmax_step(m_r, a_r, st[slot], p, shift, vt_ref[j])

    _causal_tile_walk(i, scores, softmax)
    o_ref[...] = _softmax_result(a_r).T.astype(o_ref.dtype)


def _moba_attention(proj, slopes, batch, seq, cast_weights):
    blk = MOBA_BLOCK
    t = min(MOBA_T, seq)
    nq = seq // t
    nb = seq // blk
    nh = N_MOBA_HEADS
    base = 3 * N_DIFF_HEADS
    vt = _tile_transposed_values(proj[:, (base + 2 * nh) * LANES:(base + 3 * nh) * LANES], batch, seq, nh, t)
    c_in, c_out, c_shapes, cast_from = _cast_specs(cast_weights, batch, nh, nq)
    kern = functools.partial(_moba_kernel, nb=nb, t=t, n_cast=len(cast_weights), cast_from=cast_from)
    out = pl.pallas_call(
        kern,
        grid=(batch, nh, nq),
        in_specs=[pl.BlockSpec(memory_space=pltpu.SMEM),
                  pl.BlockSpec((t, LANES), lambda b, h, i: (b * nq + i, base + h)),
                  pl.BlockSpec((seq, LANES), lambda b, h, i: (b, base + nh + h)),
                  pl.BlockSpec((None, nq, DV_ROWS, t), lambda b, h, i: (b * nh + h, 0, 0, 0))] + c_in,
        out_specs=[pl.BlockSpec((t, LANES), lambda b, h, i: (b * nq + i, h))] + c_out,
        out_shape=[jax.ShapeDtypeStruct((batch * seq, nh * LANES), BF16)] + c_shapes,
        scratch_shapes=[pltpu.VMEM((nb, LANES), BF16), pltpu.VMEM((nb, LANES), BF16), pltpu.VMEM((nb, LANES), BF16),
                        pltpu.VMEM((nb, t), F32),
                        pltpu.VMEM((1, t), F32), pltpu.VMEM((DV_ROWS, t), F32),
                        pltpu.VMEM((t, t), F32), pltpu.VMEM((t, t), F32), pltpu.VMEM((t, t), BF16),
                        pltpu.VMEM((t, t), F32), pltpu.VMEM((t, t), F32)],
        name="moba_attention",
        compiler_params=_params("arbitrary", "arbitrary", "arbitrary"),
    )(slopes, proj, proj, vt, *cast_weights)
    return out[0], out[1:]


def _swa_kernel(slopes_ref, sinks_ref, q_ref, kp_ref, kc_ref, vp_ref, vc_ref, o_ref, bias_ref, st_ref, p_ref,
                *, group):
    w = SWA_WINDOW
    b = pl.program_id(0)
    n = pl.program_id(1)
    nq = group * w
    gw = group * SWA_HEAD_DIM
    lane_head = lax.broadcasted_iota(jnp.int32, (1, nq), 1) // w
    low = lax.broadcasted_iota(jnp.int32, (w, LANES), 1) < SWA_HEAD_DIM
    upper = lax.broadcasted_iota(jnp.int32, (LANES, w), 0) < SWA_HEAD_DIM
    first = jnp.where(n > 0, 0.0, MASKED)
    strips = range(0, 2 * w, SOFTMAX_STRIP)

    def per_head(values_ref, g, scale):
        row = jnp.zeros((1, nq), F32)
        for hh in range(group):
            row = jnp.where(lane_head == hh, values_ref[g * group + hh] * scale, row)
        return row

    for gi in range(SWA_KV_PER_STEP):
        g = pl.program_id(2) * SWA_KV_PER_STEP + gi

        @pl.when(jnp.logical_and(b == 0, n == 0))
        def _():
            c = lax.broadcasted_iota(jnp.int32, (2 * w, nq), 0)
            r = lax.broadcasted_iota(jnp.int32, (2 * w, nq), 1) % w
            dist = r - c + w
            ok = jnp.logical_and(dist >= 0, dist < w)
            bias_ref[g] = jnp.where(ok, -per_head(slopes_ref, g, LOG2E) * dist.astype(F32), MASKED)

        qz = []
        for pair in range(group // 2):
            qp = q_ref[:, gi * gw + pair * LANES:gi * gw + (pair + 1) * LANES]
            qz += [jnp.where(low, qp, jnp.zeros_like(qp)), jnp.where(low, jnp.zeros_like(qp), qp)]
        qz = jnp.concatenate(qz, axis=0)
        keys = slice(gi * LANES, (gi + 1) * LANES)
        st_ref[gi, :w, :] = _nt_dot(kp_ref[:, keys], qz) + bias_ref[g, :w, :] + first
        st_ref[gi, w:, :] = _nt_dot(kc_ref[:, keys], qz) + bias_ref[g, w:, :]

    for gi in range(SWA_KV_PER_STEP):
        g = pl.program_id(2) * SWA_KV_PER_STEP + gi
        mx = None
        for r0 in strips:
            s = st_ref[gi, r0:r0 + SOFTMAX_STRIP, :]
            mx = s if mx is None else jnp.maximum(mx, s)
        sink2 = per_head(sinks_ref, g, LOG2E)
        m = jnp.maximum(jnp.max(mx, axis=0, keepdims=True), sink2)
        for r0 in strips:
            p_ref[gi, r0:r0 + SOFTMAX_STRIP, :] = jnp.exp2(st_ref[gi, r0:r0 + SOFTMAX_STRIP, :] - m).astype(p_ref.dtype)
        vt = jnp.concatenate([vp_ref[gi], vc_ref[gi]], axis=1)
        acc = jnp.dot(vt, p_ref[gi], preferred_element_type=F32)
        ot = acc[:LANES, :] / (acc[LANES:LANES + 1, :] + jnp.exp2(sink2 - m))
        for pair in range(group // 2):
            even = ot[:, (2 * pair) * w:(2 * pair + 1) * w]
            odd = ot[:, (2 * pair + 1) * w:(2 * pair + 2) * w]
            o_ref[:, gi * gw + pair * LANES:gi * gw + (pair + 1) * LANES] = (
                jnp.where(upper, even, odd).T.astype(o_ref.dtype))


def _swa_attention(proj, k2, v2t, sinks, slopes, batch, seq):
    w = SWA_WINDOW
    nblk = seq // w
    group = N_SWA_HEADS // N_SWA_KV_HEADS
    per = SWA_KV_PER_STEP
    gw = per * group * SWA_HEAD_DIM
    prev = lambda b, n, g: (b * nblk + jnp.maximum(n - 1, 0), g)
    cur = lambda b, n, g: (b * nblk + n, g)
    vblock = (None, None, per, DV_ROWS, w)
    return pl.pallas_call(
        functools.partial(_swa_kernel, group=group),
        grid=(batch, nblk, N_SWA_KV_HEADS // per),
        in_specs=[pl.BlockSpec(memory_space=pltpu.SMEM), pl.BlockSpec(memory_space=pltpu.SMEM),
                  pl.BlockSpec((w, gw), cur),
                  pl.BlockSpec((w, per * LANES), prev), pl.BlockSpec((w, per * LANES), cur),
                  pl.BlockSpec(vblock, lambda b, n, g: (b, jnp.maximum(n - 1, 0), g, 0, 0)),
                  pl.BlockSpec(vblock, lambda b, n, g: (b, n, g, 0, 0))],
        out_specs=pl.BlockSpec((w, gw), cur),
        out_shape=jax.ShapeDtypeStruct((batch * seq, N_SWA_HEADS * SWA_HEAD_DIM), BF16),
        scratch_shapes=[pltpu.VMEM((N_SWA_KV_HEADS, 2 * w, group * w), F32),
                        pltpu.VMEM((per, 2 * w, group * w), F32), pltpu.VMEM((per, 2 * w, group * w), BF16)],
        name="swa_attention",
        compiler_params=_params("arbitrary", "arbitrary", "arbitrary"),
    )(slopes, sinks, proj, k2, k2, v2t, v2t)


def _layer_norm(z, g, b):
    mu = jnp.mean(z, axis=-1, keepdims=True)
    zc = z - mu
    var = jnp.mean(zc * zc, axis=-1, keepdims=True)
    return zc * lax.rsqrt(var + LN_EPS) * g + b


def _split3(a):
    hi = a.astype(BF16)
    r1 = a - hi.astype(F32)
    mid = r1.astype(BF16)
    return hi, mid, (r1 - mid.astype(F32)).astype(BF16)


def _norm_router_kernel(x_ref, h_ref, g_ref, b_ref, rw_ref, rb_ref,
                        y_ref, eid_ref, w_ref, pos_ref, cnt_ref, carry, *, tm):
    step = pl.program_id(0)

    @pl.when(step == 0)
    def _():
        carry[...] = jnp.zeros(carry.shape, F32)

    y = _layer_norm(DEEPNORM_ALPHA * x_ref[...] + h_ref[...], g_ref[...], b_ref[...])
    y_ref[...] = y

    ne = rb_ref.shape[1]
    yh, ym, yl = _split3(y)
    dot = lambda a, b: jnp.dot(a, b, preferred_element_type=F32)
    by_hi = dot(yh, rw_ref[...])
    by_mid = dot(ym, rw_ref[:, :2 * ne])
    by_lo = dot(yl, rw_ref[:, :ne])
    logits = ((by_lo + by_mid[:, ne:] + by_hi[:, 2 * ne:]) + (by_mid[:, :ne] + by_hi[:, ne:2 * ne])
              + by_hi[:, :ne] + rb_ref[...])

    e_id = lax.broadcasted_iota(jnp.int32, logits.shape, 1)
    beaten = jnp.zeros(logits.shape, jnp.int32)
    for n in range(N_EXPERTS):
        ln = logits[:, n:n + 1]
        beaten = beaten + jnp.where(ln > logits, 1, jnp.where(ln == logits, jnp.where(e_id > n, 1, 0), 0))
    sel = beaten < TOP_K
    ex = jnp.where(sel, jnp.exp(logits - jnp.max(logits, axis=1, keepdims=True)), 0.0)
    weight = ex / jnp.sum(ex, axis=1, keepdims=True)

    self = jnp.where(sel, 1.0, 0.0)
    rr = lax.broadcasted_iota(jnp.int32, (tm, tm), 0)
    cc = lax.broadcasted_iota(jnp.int32, (tm, tm), 1)
    before = jnp.where(rr > cc, 1.0, 0.0).astype(BF16)
    pos = carry[...] + jnp.dot(before, self.astype(BF16), preferred_element_type=F32)
    carry[...] = carry[...] + jnp.sum(self, axis=0, keepdims=True)
    cnt_ref[...] = carry[...].astype(jnp.int32)

    e_f = e_id.astype(F32)
    k_id = lax.broadcasted_iota(jnp.int32, (tm, TOP_K), 1)
    e_k = jnp.zeros((tm, TOP_K), F32)
    w_k = jnp.zeros((tm, TOP_K), F32)
    p_k = jnp.zeros((tm, TOP_K), F32)
    for k in range(TOP_K):
        hit = beaten == k
        pick = lambda a: jnp.sum(jnp.where(hit, a, 0.0), axis=1, keepdims=True)
        e_k = jnp.where(k_id == k, pick(e_f), e_k)
        w_k = jnp.where(k_id == k, pick(weight), w_k)
        p_k = jnp.where(k_id == k, pick(pos), p_k)
    eid_ref[...] = e_k.astype(jnp.int32)
    w_ref[...] = w_k
    pos_ref[...] = p_k.astype(jnp.int32)


def _norm_router(x, h, g, b, rw, rb):
    t, d = x.shape
    tm = min(NORM_TM, t)
    ne = rw.shape[1]
    row = pl.BlockSpec((tm, d), lambda i: (i, 0))
    vec = pl.BlockSpec((1, d), lambda i: (0, 0))
    per_k = pl.BlockSpec((tm, TOP_K), lambda i: (i, 0))
    return pl.pallas_call(
        functools.partial(_norm_router_kernel, tm=tm),
        grid=(t // tm,),
        in_specs=[row, row, vec, vec, pl.BlockSpec((d, 3 * ne), lambda i: (0, 0)),
                  pl.BlockSpec((1, ne), lambda i: (0, 0))],
        out_specs=[row, per_k, per_k, per_k, pl.BlockSpec((1, ne), lambda i: (0, 0))],
        out_shape=[jax.ShapeDtypeStruct((t, d), F32), jax.ShapeDtypeStruct((t, TOP_K), jnp.int32),
                   jax.ShapeDtypeStruct((t, TOP_K), F32), jax.ShapeDtypeStruct((t, TOP_K), jnp.int32),
                   jax.ShapeDtypeStruct((1, ne), jnp.int32)],
        scratch_shapes=[pltpu.VMEM((1, ne), F32)],
        name="norm_router",
        compiler_params=_params("arbitrary"),
    )(x, h, g.reshape(1, d), b.reshape(1, d), jnp.concatenate(_split3(rw), axis=1), rb.reshape(1, ne))


def _row_copy(src_ref, dst_ref, src_row, dst_row, sem):
    return pltpu.make_async_copy(src_ref.at[pl.ds(src_row, 1), :], dst_ref.at[pl.ds(dst_row, 1), :], sem)


def _dispatch_kernel(slot_ref, pad_lo_ref, pad_hi_ref, nu_ref, y_ref, xs_hbm, zrow, sem, pad_sem, *, tm, ne):
    step = pl.program_id(0)

    @pl.when(step == 0)
    def _():
        zrow[...] = jnp.zeros(zrow.shape, zrow.dtype)
        tile_rows = zrow.shape[0]
        tail = lambda tl: pltpu.make_async_copy(
            zrow, xs_hbm.at[pl.ds(pl.multiple_of(tl * tile_rows, tile_rows), tile_rows), :], pad_sem)

        def fill_tile(tl, carry):
            tail(tl).start()
            return carry

        def tile_filled(tl, carry):
            tail(tl).wait()
            return carry

        lax.fori_loop(nu_ref[0], xs_hbm.shape[0] // tile_rows, fill_tile, 0)
        lax.fori_loop(nu_ref[0], xs_hbm.shape[0] // tile_rows, tile_filled, 0)
        for e in range(ne):
            lo = pad_lo_ref[e]
            hi = pad_hi_ref[e]

            def fill(r, carry):
                _row_copy(zrow, xs_hbm, 0, r, pad_sem).start()
                return carry

            def filled(r, carry):
                _row_copy(zrow, xs_hbm, 0, r, pad_sem).wait()
                return carry

            lax.fori_loop(lo, hi, fill, 0)
            lax.fori_loop(lo, hi, filled, 0)

    base = step * tm * TOP_K

    def issue(r, carry):
        for k in range(TOP_K):
            _row_copy(y_ref, xs_hbm, r, slot_ref[base + r * TOP_K + k], sem).start()
        return carry

    def drain(r, carry):
        for k in range(TOP_K):
            _row_copy(y_ref, xs_hbm, r, slot_ref[base + r * TOP_K + k], sem).wait()
        return carry

    lax.fori_loop(0, tm, issue, 0, unroll=4)
    lax.fori_loop(0, tm, drain, 0, unroll=4)


def _dispatch_rows(y, slots, pad_lo, pad_hi, n_used, n_rows):
    t, d = y.shape
    tm = min(DISPATCH_T, t)
    ne = pad_lo.shape[0]
    return pl.pallas_call(
        functools.partial(_dispatch_kernel, tm=tm, ne=ne),
        grid_spec=pltpu.PrefetchScalarGridSpec(
            num_scalar_prefetch=4,
            grid=(t // tm,),
            in_specs=[pl.BlockSpec((tm, d), lambda i, s, lo, hi, nu: (i, 0))],
            out_specs=pl.BlockSpec(memory_space=pl.ANY),
            scratch_shapes=[pltpu.VMEM((MOE_TM, d), y.dtype), pltpu.SemaphoreType.DMA(()),
                            pltpu.SemaphoreType.DMA(())]),
        out_shape=jax.ShapeDtypeStruct((n_rows, d), y.dtype),
        name="dispatch_scatter",
        compiler_params=_params("arbitrary"),
    )(slots, pad_lo, pad_hi, n_used, y)


def _gate_up_kernel(te_ref, nu_ref, x_ref, wg_ref, wu_ref, bg_ref, bu_ref, h_ref):
    t = pl.program_id(0)

    @pl.when(t < nu_ref[0])
    def _():
        x = x_ref[...].astype(BF16)
        gate = jnp.minimum(jnp.dot(x, wg_ref[0], preferred_element_type=F32) + bg_ref[0], SWIGLU_LIMIT)
        up = jnp.clip(jnp.dot(x, wu_ref[0], preferred_element_type=F32) + bu_ref[0], -SWIGLU_LIMIT, SWIGLU_LIMIT)
        h_ref[...] = ((up + 1.0) * gate * jax.nn.sigmoid(SWIGLU_ALPHA * gate)).astype(h_ref.dtype)

    @pl.when(t >= nu_ref[0])
    def _():
        h_ref[...] = jnp.zeros(h_ref.shape, h_ref.dtype)


def _down_kernel(te_ref, nu_ref, h_ref, wd_ref, bd_ref, y_ref, wd_bf):
    t = pl.program_id(0)

    @pl.when(jnp.logical_or(t == 0, te_ref[t] != te_ref[jnp.maximum(t - 1, 0)]))
    def _():
        wd_bf[...] = wd_ref[0].astype(BF16)

    @pl.when(t < nu_ref[0])
    def _():
        y_ref[...] = jnp.dot(h_ref[...], wd_bf[...], preferred_element_type=F32) + bd_ref[0]

    @pl.when(t >= nu_ref[0])
    def _():
        y_ref[...] = jnp.zeros(y_ref.shape, y_ref.dtype)


def _expert_ffn(xs, tile_expert, n_used, w_gate, b_gate, w_up, b_up, w_down, b_down):
    p, d = xs.shape
    ne, _, f = w_gate.shape
    tm = MOE_TM
    nt = p // tm
    used = lambda t, nu: jnp.minimum(t, nu[0] - 1)
    expert = lambda t, te, nu: (te[t], 0, 0)
    h = pl.pallas_call(
        _gate_up_kernel,
        grid_spec=pltpu.PrefetchScalarGridSpec(
            num_scalar_prefetch=2,
            grid=(nt,),
            in_specs=[pl.BlockSpec((tm, d), lambda t, te, nu: (used(t, nu), 0)),
                      pl.BlockSpec((1, d, f), expert), pl.BlockSpec((1, d, f), expert),
                      pl.BlockSpec((1, 1, f), expert), pl.BlockSpec((1, 1, f), expert)],
            out_specs=pl.BlockSpec((tm, f), lambda t, te, nu: (t, 0))),
        out_shape=jax.ShapeDtypeStruct((p, f), BF16),
        name="expert_gate_up",
        compiler_params=_params("arbitrary"),
    )(tile_expert, n_used, xs, w_gate, w_up, b_gate.reshape(ne, 1, f), b_up.reshape(ne, 1, f))
    return pl.pallas_call(
        _down_kernel,
        grid_spec=pltpu.PrefetchScalarGridSpec(
            num_scalar_prefetch=2,
            grid=(nt,),
            in_specs=[pl.BlockSpec((tm, f), lambda t, te, nu: (used(t, nu), 0)),
                      pl.BlockSpec((1, f, d), expert), pl.BlockSpec((1, 1, d), expert)],
            out_specs=pl.BlockSpec((tm, d), lambda t, te, nu: (t, 0)),
            scratch_shapes=[pltpu.VMEM((f, d), BF16)]),
        out_shape=jax.ShapeDtypeStruct((p, d), F32),
        name="expert_down",
        compiler_params=_params("arbitrary"),
    )(tile_expert, n_used, h, w_down, b_down.reshape(ne, 1, d))


def _combine_norm_kernel(slot_ref, x_ref, w_ref, g_ref, b_ref, ys_hbm, o_ref, obf_ref, buf, sems, *, tc):
    step = pl.program_id(0)
    cur = step % 2

    def fetch(s, half):
        base = s * tc * TOP_K

        def issue(r, carry):
            for k in range(TOP_K):
                _row_copy(ys_hbm, buf.at[half, k], slot_ref[base + r * TOP_K + k], r, sems.at[half]).start()
            return carry

        lax.fori_loop(0, tc, issue, 0, unroll=4)

    @pl.when(step == 0)
    def _():
        fetch(0, 0)

    @pl.when(step + 1 < pl.num_programs(0))
    def _():
        fetch(step + 1, 1 - cur)

    def drain(r, carry):
        for k in range(TOP_K):
            _row_copy(ys_hbm, buf.at[cur, k], 0, r, sems.at[cur]).wait()
        return carry

    lax.fori_loop(0, tc, drain, 0, unroll=4)

    w = w_ref[...]
    moe = w[:, 0:1] * buf[cur, 0]
    for k in range(1, TOP_K):
        moe = moe + w[:, k:k + 1] * buf[cur, k]
    out = _layer_norm(DEEPNORM_ALPHA * x_ref[...] + moe, g_ref[...], b_ref[...])
    o_ref[...] = out
    obf_ref[...] = out.astype(obf_ref.dtype)


def _combine_norm(x, ys, slots, w4, g, b):
    t, d = x.shape
    tc = min(COMBINE_T, t)
    row = pl.BlockSpec((tc, d), lambda i, s: (i, 0))
    return pl.pallas_call(
        functools.partial(_combine_norm_kernel, tc=tc),
        grid_spec=pltpu.PrefetchScalarGridSpec(
            num_scalar_prefetch=1,
            grid=(t // tc,),
            in_specs=[row,
                      pl.BlockSpec((tc, TOP_K), lambda i, s: (i, 0)),
                      pl.BlockSpec((1, d), lambda i, s: (0, 0)),
                      pl.BlockSpec((1, d), lambda i, s: (0, 0)),
                      pl.BlockSpec(memory_space=pl.ANY)],
            out_specs=[row, row],
            scratch_shapes=[pltpu.VMEM((2, TOP_K, tc, d), F32), pltpu.SemaphoreType.DMA((2,))]),
        out_shape=[jax.ShapeDtypeStruct((t, d), F32), jax.ShapeDtypeStruct((t, d), BF16)],
        name="combine_norm",
        compiler_params=_params("arbitrary"),
    )(slots, x, w4, g.reshape(1, d), b.reshape(1, d), ys)


def _routing_tables(eid, pos, counts, tm):
    t = eid.shape[0]
    counts = counts[0]
    ne = counts.shape[0]
    padded = ((counts + tm - 1) // tm) * tm
    ends = jnp.cumsum(padded)
    starts = ends - padded
    n_tiles = (t * TOP_K) // tm + ne
    n_used = (ends[-1] // tm).astype(jnp.int32)
    tile_start = jnp.arange(n_tiles, dtype=jnp.int32) * tm
    tile_expert = jnp.sum((tile_start[:, None] >= ends[None, :]).astype(jnp.int32), axis=1)
    tile_expert = jnp.minimum(tile_expert, ne - 1)
    last = jnp.sum(jnp.where(jnp.arange(n_tiles) == n_used - 1, tile_expert, 0))
    tile_expert = jnp.where(jnp.arange(n_tiles) < n_used, tile_expert, last).astype(jnp.int32)
    group_start = jnp.sum(jnp.where(eid[:, :, None] == jnp.arange(ne)[None, None, :], starts[None, None, :], 0), axis=2)
    slots = (group_start + pos).astype(jnp.int32).reshape(-1)
    return slots, tile_expert, n_used.reshape(1), (starts + counts).astype(jnp.int32), ends.astype(jnp.int32), n_tiles * tm


def _post_norm_moe(x, h, ln1_g, ln1_b, router_w, router_b, w_gate, b_gate, w_up, b_up, w_down, b_down,
                   ln2_g, ln2_b):
    y, eid, w4, pos, counts = _norm_router(x, h, ln1_g, ln1_b, router_w, router_b)
    slots, tile_expert, n_used, pad_lo, pad_hi, n_rows = _routing_tables(eid, pos, counts, MOE_TM)
    xs = _dispatch_rows(y, slots, pad_lo, pad_hi, n_used, n_rows)
    ys = _expert_ffn(xs, tile_expert, n_used, w_gate, b_gate, w_up, b_up, w_down, b_down)
    return _combine_norm(y, ys, slots, w4, ln2_g, ln2_b)


def kernel(x, l0_w_in, l0_lambda_q1, l0_lambda_k1, l0_lambda_q2, l0_lambda_k2, l0_subln_g, l0_w_o, l0_ln1_g, l0_ln1_b, l0_router_w, l0_router_b, l0_w_gate, l0_b_gate, l0_w_up, l0_b_up, l0_w_down, l0_b_down, l0_ln2_g, l0_ln2_b, l1_w_in, l1_sinks, l1_w_o, l1_ln1_g, l1_ln1_b, l1_router_w, l1_router_b, l1_w_gate, l1_b_gate, l1_w_up, l1_b_up, l1_w_down, l1_b_down, l1_ln2_g, l1_ln2_b):
    batch, seq, d = x.shape
    xt = x.reshape(batch * seq, d)
    flat = lambda w: w.reshape(-1, w.shape[-1])
    like = lambda wb, w: wb.reshape(w.shape)

    a_w = N_DIFF_HEADS * 2 * DIFF_QK_DIM
    b_w = N_MOBA_HEADS * LANES
    one = lambda n: jnp.ones((n,), F32)
    scale0 = jnp.concatenate([one(a_w) * (LOG2E * DIFF_QK_DIM ** -0.5), one(2 * a_w),
                              one(b_w) * (LOG2E * LANES ** -0.5), one(2 * b_w)])
    proj = _matmul(xt.astype(BF16), l0_w_in, BF16, col_scale=scale0)
    slopes = _alibi_slopes(N_DIFF_HEADS + N_MOBA_HEADS)
    lam_init = 0.8 - 0.6 * math.exp(-0.3 * 0)
    attn_a, (wg0, wu0, wo0) = _diff_attention(
        proj, l0_lambda_q1, l0_lambda_k1, l0_lambda_q2, l0_lambda_k2, l0_subln_g, slopes[0::2], batch, seq,
        lam_init, [flat(l0_w_gate), flat(l0_w_up), l0_w_o])
    attn_b, (wg1, wu1, wo1) = _moba_attention(
        proj, slopes[1::2], batch, seq, [flat(l1_w_gate), flat(l1_w_up), l1_w_o])
    h = _matmul2(attn_a, attn_b, wo0, F32)
    xt, xt_bf = _post_norm_moe(xt, h, l0_ln1_g, l0_ln1_b, l0_router_w, l0_router_b, like(wg0, l0_w_gate), l0_b_gate,
                               like(wu0, l0_w_up), l0_b_up, l0_w_down, l0_b_down, l0_ln2_g, l0_ln2_b)

    q_w = N_SWA_HEADS * SWA_HEAD_DIM
    kv_w = N_SWA_KV_HEADS * SWA_HEAD_DIM
    scale1 = jnp.concatenate([one(q_w) * (LOG2E * SWA_HEAD_DIM ** -0.5), one(2 * kv_w)])
    proj = _matmul(xt_bf, l1_w_in, BF16, col_scale=scale1)
    dup = lambda a: jnp.broadcast_to(a.reshape(-1, N_SWA_KV_HEADS, 1, SWA_HEAD_DIM),
                                     (a.shape[0], N_SWA_KV_HEADS, 2, SWA_HEAD_DIM)).reshape(a.shape[0], 2 * kv_w)
    k2 = dup(proj[:, q_w:q_w + kv_w])
    vt = proj[:, q_w + kv_w:].reshape(batch, seq // SWA_WINDOW, SWA_WINDOW, N_SWA_KV_HEADS, SWA_HEAD_DIM)
    vt = vt.transpose(0, 1, 3, 4, 2)
    fill = lambda rows, value: jnp.full(vt.shape[:3] + (rows, SWA_WINDOW), value, vt.dtype)
    v2t = jnp.concatenate([vt, vt, fill(1, 1.0), fill(DV_ROWS - LANES - 1, 0.0)], axis=3)
    attn = _swa_attention(proj, k2, v2t, l1_sinks, _alibi_slopes(N_SWA_HEADS), batch, seq)
    h = _matmul(attn, wo1, F32)
    xt, _ = _post_norm_moe(xt, h, l1_ln1_g, l1_ln1_b, l1_router_w, l1_router_b, like(wg1, l1_w_gate), l1_b_gate,
                           like(wu1, l1_w_up), l1_b_up, l1_w_down, l1_b_down, l1_ln2_g, l1_ln2_b)
    return xt.reshape(batch, seq, d)
```

```python
import functools
import math

import jax
import jax.numpy as jnp
from jax import lax
from jax.experimental import pallas as pl
from jax.experimental.pallas import tpu as pltpu

F32 = jnp.float32
BF16 = jnp.bfloat16
NEG_INF = float("-inf")
MASKED = -1e30

DEPTH = 2
DIFF_QK_DIM = 64
N_DIFF_HEADS = 16
N_MOBA_HEADS = 16
MOBA_BLOCK = 256
MOBA_TOPK = 3
N_SWA_HEADS = 64
N_SWA_KV_HEADS = 8
SWA_HEAD_DIM = 64
SWA_WINDOW = 128
N_EXPERTS = 32
TOP_K = 4
SWIGLU_LIMIT = 7.0
SWIGLU_ALPHA = 1.702
LN_EPS = 1e-5
SUBLN_EPS = 1e-5
DEEPNORM_ALPHA = (2.0 * DEPTH) ** 0.25
LOG2E = math.log2(math.e)

LANES = 128
BF16_SUBLANES = 16
DV_ROWS = LANES + BF16_SUBLANES
VMEM_LIMIT = 56 * 1024 * 1024

MM_TM, MM_TN = 512, 1024
DIFF_T = 512
MOBA_T = 512
SOFTMAX_STRIP = 32
SWA_KV_PER_STEP = 4
NORM_TM = 256
MOE_TM = 256
DISPATCH_T = 128
COMBINE_T = 128


def _params(*sem):
    return pltpu.CompilerParams(dimension_semantics=sem, vmem_limit_bytes=VMEM_LIMIT)


def _alibi_slopes(n):
    return 2.0 ** (-8.0 * (jnp.arange(n, dtype=F32) + 1.0) / n)


def _nt_dot(a, b):
    return lax.dot_general(a, b, (((1,), (1,)), ((), ())), preferred_element_type=F32)


def _matmul_kernel(x_ref, w_ref, o_ref):
    o_ref[...] = jnp.dot(x_ref[...], w_ref[...], preferred_element_type=F32).astype(o_ref.dtype)


def _scaled_matmul_kernel(x_ref, w_ref, s_ref, o_ref, wb_ref):
    @pl.when(pl.program_id(1) == 0)
    def _():
        wb_ref[...] = w_ref[...].astype(wb_ref.dtype)

    acc = jnp.dot(x_ref[...], wb_ref[...], preferred_element_type=F32)
    o_ref[...] = (acc * s_ref[...]).astype(o_ref.dtype)


def _matmul(x, w, out_dtype, col_scale=None, tm=MM_TM, tn=MM_TN):
    m, k = x.shape
    n = w.shape[1]
    tm, tn = min(tm, m), min(tn, n)
    in_specs = [pl.BlockSpec((tm, k), lambda j, i: (i, 0)), pl.BlockSpec((k, tn), lambda j, i: (0, j))]
    operands = [x, w]
    scratch = []
    if col_scale is not None:
        in_specs.append(pl.BlockSpec((1, tn), lambda j, i: (0, j)))
        operands.append(col_scale.reshape(1, n))
        scratch.append(pltpu.VMEM((k, tn), BF16))
    return pl.pallas_call(
        _matmul_kernel if col_scale is None else _scaled_matmul_kernel,
        grid=(n // tn, m // tm),
        in_specs=in_specs,
        out_specs=pl.BlockSpec((tm, tn), lambda j, i: (i, j)),
        out_shape=jax.ShapeDtypeStruct((m, n), out_dtype),
        scratch_shapes=scratch,
        name="matmul",
        compiler_params=_params("arbitrary", "arbitrary"),
    )(*operands)


def _matmul2_kernel(xa_ref, xb_ref, w_ref, o_ref, *, ka):
    acc = jnp.dot(xa_ref[...], w_ref[:ka, :], preferred_element_type=F32)
    acc = acc + jnp.dot(xb_ref[...], w_ref[ka:, :], preferred_element_type=F32)
    o_ref[...] = acc.astype(o_ref.dtype)


def _matmul2(xa, xb, w, out_dtype, tm=MM_TM, tn=MM_TN):
    m, ka = xa.shape
    kb = xb.shape[1]
    n = w.shape[1]
    tm, tn = min(tm, m), min(tn, n)
    return pl.pallas_call(
        functools.partial(_matmul2_kernel, ka=ka),
        grid=(n // tn, m // tm),
        in_specs=[pl.BlockSpec((tm, ka), lambda j, i: (i, 0)),
                  pl.BlockSpec((tm, kb), lambda j, i: (i, 0)),
                  pl.BlockSpec((ka + kb, tn), lambda j, i: (0, j))],
        out_specs=pl.BlockSpec((tm, tn), lambda j, i: (i, j)),
        out_shape=jax.ShapeDtypeStruct((m, n), out_dtype),
        name="matmul2",
        compiler_params=_params("parallel", "parallel"),
    )(xa, xb, w)


def _softmax_step(m_ref, a_ref, st_ref, p_ref, shift, vt):
    tk = st_ref.shape[0]
    strips = range(0, tk, SOFTMAX_STRIP)
    mx = None
    for r in strips:
        s = st_ref[r:r + SOFTMAX_STRIP, :]
        mx = s if mx is None else jnp.maximum(mx, s)
    m_prev = m_ref[...]
    m_next = jnp.maximum(m_prev, jnp.max(mx, axis=0, keepdims=True) - shift)
    alpha = jnp.exp2(m_prev - m_next)
    base = m_next + shift
    for r in strips:
        p_ref[r:r + SOFTMAX_STRIP, :] = jnp.exp2(st_ref[r:r + SOFTMAX_STRIP, :] - base).astype(p_ref.dtype)
    a_ref[...] = alpha * a_ref[...] + jnp.dot(vt, p_ref[...], preferred_element_type=F32)
    m_ref[...] = m_next


def _softmax_init(m_ref, a_ref):
    m_ref[...] = jnp.full(m_ref.shape, MASKED, F32)
    a_ref[...] = jnp.zeros(a_ref.shape, F32)


def _causal_tile_walk(i, scores, softmax):
    @pl.when(i == 0)
    def _():
        scores(i, 0, True)
        softmax(i, 0, True)

    @pl.when(i > 0)
    def _():
        scores(0, 0, False)
        pairs = (i - 1) // 2

        def body(jj, carry):
            a = 2 * jj
            scores(a + 1, 1, False)
            softmax(a, 0, False)
            scores(a + 2, 0, False)
            softmax(a + 1, 1, False)
            return carry

        lax.fori_loop(0, pairs, body, 0)
        a = 2 * pairs

        @pl.when(a == i - 1)
        def _():
            scores(i, 1, True)
            softmax(a, 0, False)
            softmax(i, 1, True)

        @pl.when(a == i - 2)
        def _():
            scores(a + 1, 1, False)
            softmax(a, 0, False)
            scores(i, 0, True)
            softmax(a + 1, 1, False)
            softmax(i, 0, True)


def _alibi_tiles(bias0, bias_diag, slope2, i):
    @pl.when(i == 0)
    def _():
        t = bias0.shape[0]
        rel = lax.broadcasted_iota(jnp.int32, (t, t), 1) - lax.broadcasted_iota(jnp.int32, (t, t), 0)
        bias = -slope2 * rel.astype(F32)
        bias0[...] = bias
        bias_diag[...] = jnp.where(rel >= 0, bias, MASKED)


def _softmax_result(a_ref):
    return a_ref[:LANES, :] / a_ref[LANES:LANES + 1, :]


def _cast_specs(weights, batch, nh, nq):
    first = nq // 2
    per_head = nq - first
    steps = batch * nh * per_head

    def block(b, h, i):
        return jnp.maximum((b * nh + h) * per_head + jnp.maximum(i - first, -1), 0), 0

    in_specs, out_specs, out_shapes = [], [], []
    for w in weights:
        rows, cols = w.shape
        blk = rows // steps
        assert blk * steps == rows and blk % BF16_SUBLANES == 0, (w.shape, steps)
        spec = pl.BlockSpec((blk, cols), block)
        in_specs.append(spec)
        out_specs.append(spec)
        out_shapes.append(jax.ShapeDtypeStruct(w.shape, BF16))
    return in_specs, out_specs, out_shapes, first


def _cast_blocks(src_refs, dst_refs, i, first):
    @pl.when(i >= first)
    def _():
        for src, dst in zip(src_refs, dst_refs):
            dst[...] = src[...].astype(dst.dtype)


def _diff_attn_kernel(*refs, t, lam_init, n_cast, cast_from):
    (slopes_ref, lq1_ref, lk1_ref, lq2_ref, lk2_ref, g_ref, q_ref, k_ref, vt_ref), refs = refs[:9], refs[9:]
    cast_src, (o_ref,), cast_dst = refs[:n_cast], refs[n_cast:n_cast + 1], refs[n_cast + 1:2 * n_cast + 1]
    m1, a1, m2, a2, st1a, st1b, st2a, st2b, p1, p2, bias0, bias_diag = refs[2 * n_cast + 1:]
    st1, st2 = (st1a, st1b), (st2a, st2b)
    h = pl.program_id(1)
    i = pl.program_id(2)
    _cast_blocks(cast_src, cast_dst, i, cast_from)
    slope2 = slopes_ref[h] * LOG2E
    lam = (jnp.exp(jnp.sum(lq1_ref[...] * lk1_ref[...], keepdims=True))
           - jnp.exp(jnp.sum(lq2_ref[...] * lk2_ref[...], keepdims=True)) + lam_init)

    q = q_ref[...]
    lane = lax.broadcasted_iota(jnp.int32, q.shape, 1)
    q1 = jnp.where(lane < DIFF_QK_DIM, q, jnp.zeros_like(q))
    q2 = jnp.where(lane >= DIFF_QK_DIM, q, jnp.zeros_like(q))

    _alibi_tiles(bias0, bias_diag, slope2, i)

    _softmax_init(m1, a1)
    _softmax_init(m2, a2)

    def scores(j, slot, causal):
        kt = k_ref[pl.ds(pl.multiple_of(j * t, t), t), :]
        bias = bias_diag if causal else bias0
        st1[slot][...] = _nt_dot(kt, q1) + bias[...]
        st2[slot][...] = _nt_dot(kt, q2) + bias[...]

    def softmax(j, slot, causal):
        shift = 0.0 if causal else slope2 * ((i - j) * t).astype(F32)
        vt = vt_ref[j]
        _softmax_step(m1, a1, st1[slot], p1, shift, vt)
        _softmax_step(m2, a2, st2[slot], p2, shift, vt)

    _causal_tile_walk(i, scores, softmax)

    ot = _softmax_result(a1) - lam * _softmax_result(a2)
    ot = ot * lax.rsqrt(jnp.mean(ot * ot, axis=0, keepdims=True) + SUBLN_EPS)
    o_ref[...] = (ot.T * g_ref[...] * (1.0 - lam_init)).astype(o_ref.dtype)


def _tile_transposed_values(v, batch, seq, nh, t):
    v = v.reshape(batch, seq // t, t, nh, LANES).transpose(0, 3, 1, 4, 2)
    v = v.reshape(batch * nh, seq // t, LANES, t)
    ones = jnp.ones(v.shape[:2] + (1, t), v.dtype)
    zeros = jnp.zeros(v.shape[:2] + (DV_ROWS - LANES - 1, t), v.dtype)
    return jnp.concatenate([v, ones, zeros], axis=2)


def _diff_attention(proj, lq1, lk1, lq2, lk2, subln_g, slopes, batch, seq, lam_init, cast_weights):
    t = min(DIFF_T, seq)
    nq = seq // t
    nh = N_DIFF_HEADS
    vt = _tile_transposed_values(proj[:, 2 * nh * LANES:3 * nh * LANES], batch, seq, nh, t)
    vec = lambda n: pl.BlockSpec((1, n), lambda b, h, i: (0, 0))
    c_in, c_out, c_shapes, cast_from = _cast_specs(cast_weights, batch, nh, nq)
    kern = functools.partial(_diff_attn_kernel, t=t, lam_init=lam_init, n_cast=len(cast_weights),
                             cast_from=cast_from)
    stat = pltpu.VMEM((1, t), F32)
    acc = pltpu.VMEM((DV_ROWS, t), F32)
    out = pl.pallas_call(
        kern,
        grid=(batch, nh, nq),
        in_specs=[pl.BlockSpec(memory_space=pltpu.SMEM),
                  vec(DIFF_QK_DIM), vec(DIFF_QK_DIM), vec(DIFF_QK_DIM), vec(DIFF_QK_DIM), vec(LANES),
                  pl.BlockSpec((t, LANES), lambda b, h, i: (b * nq + i, h)),
                  pl.BlockSpec((seq, LANES), lambda b, h, i: (b, nh + h)),
                  pl.BlockSpec((None, nq, DV_ROWS, t), lambda b, h, i: (b * nh + h, 0, 0, 0))] + c_in,
        out_specs=[pl.BlockSpec((t, LANES), lambda b, h, i: (b * nq + i, h))] + c_out,
        out_shape=[jax.ShapeDtypeStruct((batch * seq, nh * LANES), BF16)] + c_shapes,
        scratch_shapes=[stat, acc, stat, acc] + [pltpu.VMEM((t, t), F32)] * 4
                       + [pltpu.VMEM((t, t), BF16)] * 2 + [pltpu.VMEM((t, t), F32)] * 2,
        name="diff_attention",
        compiler_params=_params("arbitrary", "arbitrary", "arbitrary"),
    )(slopes, lq1.reshape(1, -1), lk1.reshape(1, -1), lq2.reshape(1, -1), lk2.reshape(1, -1),
      subln_g.reshape(1, -1), proj, proj, vt, *cast_weights)
    return out[0], out[1:]


def _moba_kernel(*refs, nb, t, n_cast, cast_from):
    (slopes_ref, q_ref, k_ref, vt_ref), refs = refs[:4], refs[4:]
    cast_src, (o_ref,), cast_dst = refs[:n_cast], refs[n_cast:n_cast + 1], refs[n_cast + 1:2 * n_cast + 1]
    km_hi, km_mid, km_lo, selb, m_r, a_r, st_a, st_b, p, bias0, bias_diag = refs[2 * n_cast + 1:]
    st = (st_a, st_b)

    blk = MOBA_BLOCK
    per = t // blk
    h = pl.program_id(1)
    i = pl.program_id(2)
    slope2 = slopes_ref[h] * LOG2E
    _cast_blocks(cast_src, cast_dst, i, cast_from)

    @pl.when(i == 0)
    def _():
        km = jnp.sum(k_ref[...].astype(F32).reshape(nb, blk, LANES), axis=1) * (1.0 / blk)
        hi = km.astype(BF16)
        r1 = km - hi.astype(F32)
        mid = r1.astype(BF16)
        km_hi[...] = hi
        km_mid[...] = mid
        km_lo[...] = (r1 - mid.astype(F32)).astype(BF16)

    q = q_ref[...]
    gate = _nt_dot(km_hi[...], q) + _nt_dot(km_mid[...], q) + _nt_dot(km_lo[...], q)
    blk_id = lax.broadcasted_iota(jnp.int32, gate.shape, 0)
    q_sub = lax.broadcasted_iota(jnp.int32, (1, t), 1) // blk
    past = blk_id < i * per + q_sub
    gate = jnp.where(past, gate, NEG_INF)
    beaten = jnp.zeros(gate.shape, jnp.int32)
    for n in range(nb):
        gn = gate[n:n + 1, :]
        beaten = beaten + jnp.where(gn > gate, 1, jnp.where(gn == gate, jnp.where(blk_id > n, 1, 0), 0))
    selb[...] = jnp.where(past, jnp.where(beaten < MOBA_TOPK, 0.0, MASKED), MASKED)

    _alibi_tiles(bias0, bias_diag, slope2, i)
    _softmax_init(m_r, a_r)

    def scores(j, slot, own):
        rows = []
        for u in range(per):
            row = selb[pl.ds(j * per + u, 1), :]
            if own:
                row = jnp.where(q_sub <= u, 0.0, row)
            rows.append(jnp.broadcast_to(row, (blk, t)))
        bias = bias_diag if own else bias0
        st[slot][...] = (_nt_dot(k_ref[pl.ds(pl.multiple_of(j * t, t), t), :], q) + bias[...]
                         + jnp.concatenate(rows, axis=0))

    def softmax(j, slot, own):
        shift = 0.0 if own else slope2 * ((i - j) * t).astype(F32)
        _softmax_step(m_r, a_r, st[slot], p, shift, vt_ref[j])

    _causal_tile_walk(i, scores, softmax)
    o_ref[...] = _softmax_result(a_r).T.astype(o_ref.dtype)


def _moba_attention(proj, slopes, batch, seq, cast_weights):
    blk = MOBA_BLOCK
    t = min(MOBA_T, seq)
    nq = seq // t
    nb = seq // blk
    nh = N_MOBA_HEADS
    base = 3 * N_DIFF_HEADS
    vt = _tile_transposed_values(proj[:, (base + 2 * nh) * LANES:(base + 3 * nh) * LANES], batch, seq, nh, t)
    c_in, c_out, c_shapes, cast_from = _cast_specs(cast_weights, batch, nh, nq)
    kern = functools.partial(_moba_kernel, nb=nb, t=t, n_cast=len(cast_weights), cast_from=cast_from)
    out = pl.pallas_call(
        kern,
        grid=(batch, nh, nq),
        in_specs=[pl.BlockSpec(memory_space=pltpu.SMEM),
                  pl.BlockSpec((t, LANES), lambda b, h, i: (b * nq + i, base + h)),
                  pl.BlockSpec((seq, LANES), lambda b, h, i: (b, base + nh + h)),
                  pl.BlockSpec((None, nq, DV_ROWS, t), lambda b, h, i: (b * nh + h, 0, 0, 0))] + c_in,
        out_specs=[pl.BlockSpec((t, LANES), lambda b, h, i: (b * nq + i, h))] + c_out,
        out_shape=[jax.ShapeDtypeStruct((batch * seq, nh * LANES), BF16)] + c_shapes,
        scratch_shapes=[pltpu.VMEM((nb, LANES), BF16), pltpu.VMEM((nb, LANES), BF16), pltpu.VMEM((nb, LANES), BF16),
                        pltpu.VMEM((nb, t), F32),
                        pltpu.VMEM((1, t), F32), pltpu.VMEM((DV_ROWS, t), F32),
                        pltpu.VMEM((t, t), F32), pltpu.VMEM((t, t), F32), pltpu.VMEM((t, t), BF16),
                        pltpu.VMEM((t, t), F32), pltpu.VMEM((t, t), F32)],
        name="moba_attention",
        compiler_params=_params("arbitrary", "arbitrary", "arbitrary"),
    )(slopes, proj, proj, vt, *cast_weights)
    return out[0], out[1:]


def _swa_kernel(slopes_ref, sinks_ref, q_ref, kp_ref, kc_ref, vp_ref, vc_ref, o_ref, bias_ref, st_ref, p_ref,
                *, group):
    w = SWA_WINDOW
    b = pl.program_id(0)
    n = pl.program_id(1)
    nq = group * w
    gw = group * SWA_HEAD_DIM
    lane_head = lax.broadcasted_iota(jnp.int32, (1, nq), 1) // w
    low = lax.broadcasted_iota(jnp.int32, (w, LANES), 1) < SWA_HEAD_DIM
    upper = lax.broadcasted_iota(jnp.int32, (LANES, w), 0) < SWA_HEAD_DIM
    first = jnp.where(n > 0, 0.0, MASKED)
    strips = range(0, 2 * w, SOFTMAX_STRIP)

    def per_head(values_ref, g, scale):
        row = jnp.zeros((1, nq), F32)
        for hh in range(group):
            row = jnp.where(lane_head == hh, values_ref[g * group + hh] * scale, row)
        return row

    for gi in range(SWA_KV_PER_STEP):
        g = pl.program_id(2) * SWA_KV_PER_STEP + gi

        @pl.when(jnp.logical_and(b == 0, n == 0))
        def _():
            c = lax.broadcasted_iota(jnp.int32, (2 * w, nq), 0)
            r = lax.broadcasted_iota(jnp.int32, (2 * w, nq), 1) % w
            dist = r - c + w
            ok = jnp.logical_and(dist >= 0, dist < w)
            bias_ref[g] = jnp.where(ok, -per_head(slopes_ref, g, LOG2E) * dist.astype(F32), MASKED)

        qz = []
        for pair in range(group // 2):
            qp = q_ref[:, gi * gw + pair * LANES:gi * gw + (pair + 1) * LANES]
            qz += [jnp.where(low, qp, jnp.zeros_like(qp)), jnp.where(low, jnp.zeros_like(qp), qp)]
        qz = jnp.concatenate(qz, axis=0)
        keys = slice(gi * LANES, (gi + 1) * LANES)
        st_ref[gi, :w, :] = _nt_dot(kp_ref[:, keys], qz) + bias_ref[g, :w, :] + first
        st_ref[gi, w:, :] = _nt_dot(kc_ref[:, keys], qz) + bias_ref[g, w:, :]

    for gi in range(SWA_KV_PER_STEP):
        g = pl.program_id(2) * SWA_KV_PER_STEP + gi
        mx = None
        for r0 in strips:
            s = st_ref[gi, r0:r0 + SOFTMAX_STRIP, :]
            mx = s if mx is None else jnp.maximum(mx, s)
        sink2 = per_head(sinks_ref, g, LOG2E)
        m = jnp.maximum(jnp.max(mx, axis=0, keepdims=True), sink2)
        for r0 in strips:
            p_ref[gi, r0:r0 + SOFTMAX_STRIP, :] = jnp.exp2(st_ref[gi, r0:r0 + SOFTMAX_STRIP, :] - m).astype(p_ref.dtype)
        vt = jnp.concatenate([vp_ref[gi], vc_ref[gi]], axis=1)
        acc = jnp.dot(vt, p_ref[gi], preferred_element_type=F32)
        ot = acc[:LANES, :] / (acc[LANES:LANES + 1, :] + jnp.exp2(sink2 - m))
        for pair in range(group // 2):
            even = ot[:, (2 * pair) * w:(2 * pair + 1) * w]
            odd = ot[:, (2 * pair + 1) * w:(2 * pair + 2) * w]
            o_ref[:, gi * gw + pair * LANES:gi * gw + (pair + 1) * LANES] = (
                jnp.where(upper, even, odd).T.astype(o_ref.dtype))


def _swa_attention(proj, k2, v2t, sinks, slopes, batch, seq):
    w = SWA_WINDOW
    nblk = seq // w
    group = N_SWA_HEADS // N_SWA_KV_HEADS
    per = SWA_KV_PER_STEP
    gw = per * group * SWA_HEAD_DIM
    prev = lambda b, n, g: (b * nblk + jnp.maximum(n - 1, 0), g)
    cur = lambda b, n, g: (b * nblk + n, g)
    vblock = (None, None, per, DV_ROWS, w)
    return pl.pallas_call(
        functools.partial(_swa_kernel, group=group),
        grid=(batch, nblk, N_SWA_KV_HEADS // per),
        in_specs=[pl.BlockSpec(memory_space=pltpu.SMEM), pl.BlockSpec(memory_space=pltpu.SMEM),
                  pl.BlockSpec((w, gw), cur),
                  pl.BlockSpec((w, per * LANES), prev), pl.BlockSpec((w, per * LANES), cur),
                  pl.BlockSpec(vblock, lambda b, n, g: (b, jnp.maximum(n - 1, 0), g, 0, 0)),
                  pl.BlockSpec(vblock, lambda b, n, g: (b, n, g, 0, 0))],
        out_specs=pl.BlockSpec((w, gw), cur),
        out_shape=jax.ShapeDtypeStruct((batch * seq, N_SWA_HEADS * SWA_HEAD_DIM), BF16),
        scratch_shapes=[pltpu.VMEM((N_SWA_KV_HEADS, 2 * w, group * w), F32),
                        pltpu.VMEM((per, 2 * w, group * w), F32), pltpu.VMEM((per, 2 * w, group * w), BF16)],
        name="swa_attention",
        compiler_params=_params("arbitrary", "arbitrary", "arbitrary"),
    )(slopes, sinks, proj, k2, k2, v2t, v2t)


def _layer_norm(z, g, b):
    mu = jnp.mean(z, axis=-1, keepdims=True)
    zc = z - mu
    var = jnp.mean(zc * zc, axis=-1, keepdims=True)
    return zc * lax.rsqrt(var + LN_EPS) * g + b


def _split3(a):
    hi = a.astype(BF16)
    r1 = a - hi.astype(F32)
    mid = r1.astype(BF16)
    return hi, mid, (r1 - mid.astype(F32)).astype(BF16)


def _norm_router_kernel(x_ref, h_ref, g_ref, b_ref, rw_ref, rb_ref,
                        y_ref, eid_ref, w_ref, pos_ref, cnt_ref, carry, *, tm):
    step = pl.program_id(0)

    @pl.when(step == 0)
    def _():
        carry[...] = jnp.zeros(carry.shape, F32)

    y = _layer_norm(DEEPNORM_ALPHA * x_ref[...] + h_ref[...], g_ref[...], b_ref[...])
    y_ref[...] = y

    ne = rb_ref.shape[1]
    yh, ym, yl = _split3(y)
    dot = lambda a, b: jnp.dot(a, b, preferred_element_type=F32)
    by_hi = dot(yh, rw_ref[...])
    by_mid = dot(ym, rw_ref[:, :2 * ne])
    by_lo = dot(yl, rw_ref[:, :ne])
    logits = ((by_lo + by_mid[:, ne:] + by_hi[:, 2 * ne:]) + (by_mid[:, :ne] + by_hi[:, ne:2 * ne])
              + by_hi[:, :ne] + rb_ref[...])

    e_id = lax.broadcasted_iota(jnp.int32, logits.shape, 1)
    beaten = jnp.zeros(logits.shape, jnp.int32)
    for n in range(N_EXPERTS):
        ln = logits[:, n:n + 1]
        beaten = beaten + jnp.where(ln > logits, 1, jnp.where(ln == logits, jnp.where(e_id > n, 1, 0), 0))
    sel = beaten < TOP_K
    ex = jnp.where(sel, jnp.exp(logits - jnp.max(logits, axis=1, keepdims=True)), 0.0)
    weight = ex / jnp.sum(ex, axis=1, keepdims=True)

    self = jnp.where(sel, 1.0, 0.0)
    rr = lax.broadcasted_iota(jnp.int32, (tm, tm), 0)
    cc = lax.broadcasted_iota(jnp.int32, (tm, tm), 1)
    before = jnp.where(rr > cc, 1.0, 0.0).astype(BF16)
    pos = carry[...] + jnp.dot(before, self.astype(BF16), preferred_element_type=F32)
    carry[...] = carry[...] + jnp.sum(self, axis=0, keepdims=True)
    cnt_ref[...] = carry[...].astype(jnp.int32)

    e_f = e_id.astype(F32)
    k_id = lax.broadcasted_iota(jnp.int32, (tm, TOP_K), 1)
    e_k = jnp.zeros((tm, TOP_K), F32)
    w_k = jnp.zeros((tm, TOP_K), F32)
    p_k = jnp.zeros((tm, TOP_K), F32)
    for k in range(TOP_K):
        hit = beaten == k
        pick = lambda a: jnp.sum(jnp.where(hit, a, 0.0), axis=1, keepdims=True)
        e_k = jnp.where(k_id == k, pick(e_f), e_k)
        w_k = jnp.where(k_id == k, pick(weight), w_k)
        p_k = jnp.where(k_id == k, pick(pos), p_k)
    eid_ref[...] = e_k.astype(jnp.int32)
    w_ref[...] = w_k
    pos_ref[...] = p_k.astype(jnp.int32)


def _norm_router(x, h, g, b, rw, rb):
    t, d = x.shape
    tm = min(NORM_TM, t)
    ne = rw.shape[1]
    row = pl.BlockSpec((tm, d), lambda i: (i, 0))
    vec = pl.BlockSpec((1, d), lambda i: (0, 0))
    per_k = pl.BlockSpec((tm, TOP_K), lambda i: (i, 0))
    return pl.pallas_call(
        functools.partial(_norm_router_kernel, tm=tm),
        grid=(t // tm,),
        in_specs=[row, row, vec, vec, pl.BlockSpec((d, 3 * ne), lambda i: (0, 0)),
                  pl.BlockSpec((1, ne), lambda i: (0, 0))],
        out_specs=[row, per_k, per_k, per_k, pl.BlockSpec((1, ne), lambda i: (0, 0))],
        out_shape=[jax.ShapeDtypeStruct((t, d), F32), jax.ShapeDtypeStruct((t, TOP_K), jnp.int32),
                   jax.ShapeDtypeStruct((t, TOP_K), F32), jax.ShapeDtypeStruct((t, TOP_K), jnp.int32),
                   jax.ShapeDtypeStruct((1, ne), jnp.int32)],
        scratch_shapes=[pltpu.VMEM((1, ne), F32)],
        name="norm_router",
        compiler_params=_params("arbitrary"),
    )(x, h, g.reshape(1, d), b.reshape(1, d), jnp.concatenate(_split3(rw), axis=1), rb.reshape(1, ne))


def _row_copy(src_ref, dst_ref, src_row, dst_row, sem):
    return pltpu.make_async_copy(src_ref.at[pl.ds(src_row, 1), :], dst_ref.at[pl.ds(dst_row, 1), :], sem)


def _dispatch_kernel(slot_ref, pad_lo_ref, pad_hi_ref, nu_ref, y_ref, xs_hbm, zrow, sem, pad_sem, *, tm, ne):
    step = pl.program_id(0)

    @pl.when(step == 0)
    def _():
        zrow[...] = jnp.zeros(zrow.shape, zrow.dtype)
        tile_rows = zrow.shape[0]
        tail = lambda tl: pltpu.make_async_copy(
            zrow, xs_hbm.at[pl.ds(pl.multiple_of(tl * tile_rows, tile_rows), tile_rows), :], pad_sem)

        def fill_tile(tl, carry):
            tail(tl).start()
            return carry

        def tile_filled(tl, carry):
            tail(tl).wait()
            return carry

        lax.fori_loop(nu_ref[0], xs_hbm.shape[0] // tile_rows, fill_tile, 0)
        lax.fori_loop(nu_ref[0], xs_hbm.shape[0] // tile_rows, tile_filled, 0)
        for e in range(ne):
            lo = pad_lo_ref[e]
            hi = pad_hi_ref[e]

            def fill(r, carry):
                _row_copy(zrow, xs_hbm, 0, r, pad_sem).start()
                return carry

            def filled(r, carry):
                _row_copy(zrow, xs_hbm, 0, r, pad_sem).wait()
                return carry

            lax.fori_loop(lo, hi, fill, 0)
            lax.fori_loop(lo, hi, filled, 0)

    base = step * tm * TOP_K

    def issue(r, carry):
        for k in range(TOP_K):
            _row_copy(y_ref, xs_hbm, r, slot_ref[base + r * TOP_K + k], sem).start()
        return carry

    def drain(r, carry):
        for k in range(TOP_K):
            _row_copy(y_ref, xs_hbm, 0, 0, sem).wait()
        return carry

    lax.fori_loop(0, tm, issue, 0, unroll=4)
    lax.fori_loop(0, tm, drain, 0, unroll=16)


def _dispatch_rows(y, slots, pad_lo, pad_hi, n_used, n_rows):
    t, d = y.shape
    tm = min(DISPATCH_T, t)
    ne = pad_lo.shape[0]
    return pl.pallas_call(
        functools.partial(_dispatch_kernel, tm=tm, ne=ne),
        grid_spec=pltpu.PrefetchScalarGridSpec(
            num_scalar_prefetch=4,
            grid=(t // tm,),
            in_specs=[pl.BlockSpec((tm, d), lambda i, s, lo, hi, nu: (i, 0))],
            out_specs=pl.BlockSpec(memory_space=pl.ANY),
            scratch_shapes=[pltpu.VMEM((MOE_TM, d), y.dtype), pltpu.SemaphoreType.DMA(()),
                            pltpu.SemaphoreType.DMA(())]),
        out_shape=jax.ShapeDtypeStruct((n_rows, d), y.dtype),
        name="dispatch_scatter",
        compiler_params=_params("arbitrary"),
    )(slots, pad_lo, pad_hi, n_used, y)


def _gate_up_kernel(te_ref, nu_ref, x_ref, wg_ref, wu_ref, bg_ref, bu_ref, h_ref):
    t = pl.program_id(0)

    @pl.when(t < nu_ref[0])
    def _():
        x = x_ref[...].astype(BF16)
        gate = jnp.minimum(jnp.dot(x, wg_ref[0], preferred_element_type=F32) + bg_ref[0], SWIGLU_LIMIT)
        up = jnp.clip(jnp.dot(x, wu_ref[0], preferred_element_type=F32) + bu_ref[0], -SWIGLU_LIMIT, SWIGLU_LIMIT)
        h_ref[...] = ((up + 1.0) * gate * jax.nn.sigmoid(SWIGLU_ALPHA * gate)).astype(h_ref.dtype)

    @pl.when(t >= nu_ref[0])
    def _():
        h_ref[...] = jnp.zeros(h_ref.shape, h_ref.dtype)


def _down_kernel(te_ref, nu_ref, h_ref, wd_ref, bd_ref, y_ref, wd_bf):
    t = pl.program_id(0)

    @pl.when(jnp.logical_or(t == 0, te_ref[t] != te_ref[jnp.maximum(t - 1, 0)]))
    def _():
        wd_bf[...] = wd_ref[0].astype(BF16)

    @pl.when(t < nu_ref[0])
    def _():
        y_ref[...] = jnp.dot(h_ref[...], wd_bf[...], preferred_element_type=F32) + bd_ref[0]

    @pl.when(t >= nu_ref[0])
    def _():
        y_ref[...] = jnp.zeros(y_ref.shape, y_ref.dtype)


def _expert_ffn(xs, tile_expert, n_used, w_gate, b_gate, w_up, b_up, w_down, b_down):
    p, d = xs.shape
    ne, _, f = w_gate.shape
    tm = MOE_TM
    nt = p // tm
    used = lambda t, nu: jnp.minimum(t, nu[0] - 1)
    expert = lambda t, te, nu: (te[t], 0, 0)
    h = pl.pallas_call(
        _gate_up_kernel,
        grid_spec=pltpu.PrefetchScalarGridSpec(
            num_scalar_prefetch=2,
            grid=(nt,),
            in_specs=[pl.BlockSpec((tm, d), lambda t, te, nu: (used(t, nu), 0)),
                      pl.BlockSpec((1, d, f), expert), pl.BlockSpec((1, d, f), expert),
                      pl.BlockSpec((1, 1, f), expert), pl.BlockSpec((1, 1, f), expert)],
            out_specs=pl.BlockSpec((tm, f), lambda t, te, nu: (t, 0))),
        out_shape=jax.ShapeDtypeStruct((p, f), BF16),
        name="expert_gate_up",
        compiler_params=_params("arbitrary"),
    )(tile_expert, n_used, xs, w_gate, w_up, b_gate.reshape(ne, 1, f), b_up.reshape(ne, 1, f))
    return pl.pallas_call(
        _down_kernel,
        grid_spec=pltpu.PrefetchScalarGridSpec(
            num_scalar_prefetch=2,
            grid=(nt,),
            in_specs=[pl.BlockSpec((tm, f), lambda t, te, nu: (used(t, nu), 0)),
                      pl.BlockSpec((1, f, d), expert), pl.BlockSpec((1, 1, d), expert)],
            out_specs=pl.BlockSpec((tm, d), lambda t, te, nu: (t, 0)),
            scratch_shapes=[pltpu.VMEM((f, d), BF16)]),
        out_shape=jax.ShapeDtypeStruct((p, d), F32),
        name="expert_down",
        compiler_params=_params("arbitrary"),
    )(tile_expert, n_used, h, w_down, b_down.reshape(ne, 1, d))


def _combine_norm_kernel(slot_ref, x_ref, w_ref, g_ref, b_ref, ys_hbm, o_ref, obf_ref, buf, sems, *, tc):
    step = pl.program_id(0)
    cur = step % 2

    def fetch(s, half):
        base = s * tc * TOP_K

        def issue(r, carry):
            for k in range(TOP_K):
                _row_copy(ys_hbm, buf.at[half, k], slot_ref[base + r * TOP_K + k], r, sems.at[half]).start()
            return carry

        lax.fori_loop(0, tc, issue, 0, unroll=4)

    @pl.when(step == 0)
    def _():
        fetch(0, 0)

    @pl.when(step + 1 < pl.num_programs(0))
    def _():
        fetch(step + 1, 1 - cur)

    def drain(r, carry):
        for k in range(TOP_K):
            _row_copy(ys_hbm, buf.at[cur, 0], 0, 0, sems.at[cur]).wait()
        return carry

    lax.fori_loop(0, tc, drain, 0, unroll=16)

    w = w_ref[...]
    moe = w[:, 0:1] * buf[cur, 0]
    for k in range(1, TOP_K):
        moe = moe + w[:, k:k + 1] * buf[cur, k]
    out = _layer_norm(DEEPNORM_ALPHA * x_ref[...] + moe, g_ref[...], b_ref[...])
    o_ref[...] = out
    obf_ref[...] = out.astype(obf_ref.dtype)


def _combine_norm(x, ys, slots, w4, g, b):
    t, d = x.shape
    tc = min(COMBINE_T, t)
    row = pl.BlockSpec((tc, d), lambda i, s: (i, 0))
    return pl.pallas_call(
        functools.partial(_combine_norm_kernel, tc=tc),
        grid_spec=pltpu.PrefetchScalarGridSpec(
            num_scalar_prefetch=1,
            grid=(t // tc,),
            in_specs=[row,
                      pl.BlockSpec((tc, TOP_K), lambda i, s: (i, 0)),
                      pl.BlockSpec((1, d), lambda i, s: (0, 0)),
                      pl.BlockSpec((1, d), lambda i, s: (0, 0)),
                      pl.BlockSpec(memory_space=pl.ANY)],
            out_specs=[row, row],
            scratch_shapes=[pltpu.VMEM((2, TOP_K, tc, d), F32), pltpu.SemaphoreType.DMA((2,))]),
        out_shape=[jax.ShapeDtypeStruct((t, d), F32), jax.ShapeDtypeStruct((t, d), BF16)],
        name="combine_norm",
        compiler_params=_params("arbitrary"),
    )(slots, x, w4, g.reshape(1, d), b.reshape(1, d), ys)


def _routing_tables(eid, pos, counts, tm):
    t = eid.shape[0]
    counts = counts[0]
    ne = counts.shape[0]
    padded = ((counts + tm - 1) // tm) * tm
    ends = jnp.cumsum(padded)
    starts = ends - padded
    n_tiles = (t * TOP_K) // tm + ne
    n_used = (ends[-1] // tm).astype(jnp.int32)
    tile_start = jnp.arange(n_tiles, dtype=jnp.int32) * tm
    tile_expert = jnp.sum((tile_start[:, None] >= ends[None, :]).astype(jnp.int32), axis=1)
    tile_expert = jnp.minimum(tile_expert, ne - 1)
    last = jnp.sum(jnp.where(jnp.arange(n_tiles) == n_used - 1, tile_expert, 0))
    tile_expert = jnp.where(jnp.arange(n_tiles) < n_used, tile_expert, last).astype(jnp.int32)
    group_start = jnp.sum(jnp.where(eid[:, :, None] == jnp.arange(ne)[None, None, :], starts[None, None, :], 0), axis=2)
    slots = (group_start + pos).astype(jnp.int32).reshape(-1)
    return slots, tile_expert, n_used.reshape(1), (starts + counts).astype(jnp.int32), ends.astype(jnp.int32), n_tiles * tm


def _post_norm_moe(x, h, ln1_g, ln1_b, router_w, router_b, w_gate, b_gate, w_up, b_up, w_down, b_down,
                   ln2_g, ln2_b):
    y, eid, w4, pos, counts = _norm_router(x, h, ln1_g, ln1_b, router_w, router_b)
    slots, tile_expert, n_used, pad_lo, pad_hi, n_rows = _routing_tables(eid, pos, counts, MOE_TM)
    xs = _dispatch_rows(y, slots, pad_lo, pad_hi, n_used, n_rows)
    ys = _expert_ffn(xs, tile_expert, n_used, w_gate, b_gate, w_up, b_up, w_down, b_down)
    return _combine_norm(y, ys, slots, w4, ln2_g, ln2_b)


def kernel(x, l0_w_in, l0_lambda_q1, l0_lambda_k1, l0_lambda_q2, l0_lambda_k2, l0_subln_g, l0_w_o, l0_ln1_g, l0_ln1_b, l0_router_w, l0_router_b, l0_w_gate, l0_b_gate, l0_w_up, l0_b_up, l0_w_down, l0_b_down, l0_ln2_g, l0_ln2_b, l1_w_in, l1_sinks, l1_w_o, l1_ln1_g, l1_ln1_b, l1_router_w, l1_router_b, l1_w_gate, l1_b_gate, l1_w_up, l1_b_up, l1_w_down, l1_b_down, l1_ln2_g, l1_ln2_b):
    batch, seq, d = x.shape
    xt = x.reshape(batch * seq, d)
    flat = lambda w: w.reshape(-1, w.shape[-1])
    like = lambda wb, w: wb.reshape(w.shape)

    a_w = N_DIFF_HEADS * 2 * DIFF_QK_DIM
    b_w = N_MOBA_HEADS * LANES
    one = lambda n: jnp.ones((n,), F32)
    scale0 = jnp.concatenate([one(a_w) * (LOG2E * DIFF_QK_DIM ** -0.5), one(2 * a_w),
                              one(b_w) * (LOG2E * LANES ** -0.5), one(2 * b_w)])
    proj = _matmul(xt.astype(BF16), l0_w_in, BF16, col_scale=scale0)
    slopes = _alibi_slopes(N_DIFF_HEADS + N_MOBA_HEADS)
    lam_init = 0.8 - 0.6 * math.exp(-0.3 * 0)
    attn_a, (wg0, wu0, wo0) = _diff_attention(
        proj, l0_lambda_q1, l0_lambda_k1, l0_lambda_q2, l0_lambda_k2, l0_subln_g, slopes[0::2], batch, seq,
        lam_init, [flat(l0_w_gate), flat(l0_w_up), l0_w_o])
    attn_b, (wg1, wu1, wo1) = _moba_attention(
        proj, slopes[1::2], batch, seq, [flat(l1_w_gate), flat(l1_w_up), l1_w_o])
    h = _matmul2(attn_a, attn_b, wo0, F32)
    xt, xt_bf = _post_norm_moe(xt, h, l0_ln1_g, l0_ln1_b, l0_router_w, l0_router_b, like(wg0, l0_w_gate), l0_b_gate,
                               like(wu0, l0_w_up), l0_b_up, l0_w_down, l0_b_down, l0_ln2_g, l0_ln2_b)

    q_w = N_SWA_HEADS * SWA_HEAD_DIM
    kv_w = N_SWA_KV_HEADS * SWA_HEAD_DIM
    scale1 = jnp.concatenate([one(q_w) * (LOG2E * SWA_HEAD_DIM ** -0.5), one(2 * kv_w)])
    proj = _matmul(xt_bf, l1_w_in, BF16, col_scale=scale1)
    dup = lambda a: jnp.broadcast_to(a.reshape(-1, N_SWA_KV_HEADS, 1, SWA_HEAD_DIM),
                                     (a.shape[0], N_SWA_KV_HEADS, 2, SWA_HEAD_DIM)).reshape(a.shape[0], 2 * kv_w)
    k2 = dup(proj[:, q_w:q_w + kv_w])
    vt = proj[:, q_w + kv_w:].reshape(batch, seq // SWA_WINDOW, SWA_WINDOW, N_SWA_KV_HEADS, SWA_HEAD_DIM)
    vt = vt.transpose(0, 1, 3, 4, 2)
    fill = lambda rows, value: jnp.full(vt.shape[:3] + (rows, SWA_WINDOW), value, vt.dtype)
    v2t = jnp.concatenate([vt, vt, fill(1, 1.0), fill(DV_ROWS - LANES - 1, 0.0)], axis=3)
    attn = _swa_attention(proj, k2, v2t, l1_sinks, _alibi_slopes(N_SWA_HEADS), batch, seq)
    h = _matmul(attn, wo1, F32)
    xt, _ = _post_norm_moe(xt, h, l1_ln1_g, l1_ln1_b, l1_router_w, l1_router_b, like(wg1, l1_w_gate), l1_b_gate,
                           like(wu1, l1_w_up), l1_b_up, l1_w_down, l1_b_down, l1_ln2_g, l1_ln2_b)
    return xt.reshape(batch, seq, d)
```

```python
import functools
import math

import jax
import jax.numpy as jnp
from jax import lax
from jax.experimental import pallas as pl
from jax.experimental.pallas import tpu as pltpu

F32 = jnp.float32
BF16 = jnp.bfloat16
NEG_INF = float("-inf")
MASKED = -1e30

DEPTH = 2
DIFF_QK_DIM = 64
N_DIFF_HEADS = 16
N_MOBA_HEADS = 16
MOBA_BLOCK = 256
MOBA_TOPK = 3
N_SWA_HEADS = 64
N_SWA_KV_HEADS = 8
SWA_HEAD_DIM = 64
SWA_WINDOW = 128
N_EXPERTS = 32
TOP_K = 4
SWIGLU_LIMIT = 7.0
SWIGLU_ALPHA = 1.702
LN_EPS = 1e-5
SUBLN_EPS = 1e-5
DEEPNORM_ALPHA = (2.0 * DEPTH) ** 0.25
LOG2E = math.log2(math.e)

LANES = 128
BF16_SUBLANES = 16
DV_ROWS = LANES + BF16_SUBLANES
VMEM_LIMIT = 56 * 1024 * 1024

MM_TM, MM_TN = 512, 1024
DIFF_T = 512
MOBA_T = 512
SOFTMAX_STRIP = 32
SWA_KV_PER_STEP = 4
NORM_TM = 256
MOE_TM = 256
DISPATCH_T = 128
COMBINE_T = 128


def _params(*sem):
    return pltpu.CompilerParams(dimension_semantics=sem, vmem_limit_bytes=VMEM_LIMIT)


def _alibi_slopes(n):
    return 2.0 ** (-8.0 * (jnp.arange(n, dtype=F32) + 1.0) / n)


def _nt_dot(a, b):
    return lax.dot_general(a, b, (((1,), (1,)), ((), ())), preferred_element_type=F32)


def _matmul_kernel(x_ref, w_ref, o_ref):
    o_ref[...] = jnp.dot(x_ref[...], w_ref[...], preferred_element_type=F32).astype(o_ref.dtype)


def _scaled_matmul_kernel(x_ref, w_ref, s_ref, o_ref, wb_ref):
    @pl.when(pl.program_id(1) == 0)
    def _():
        wb_ref[...] = w_ref[...].astype(wb_ref.dtype)

    acc = jnp.dot(x_ref[...], wb_ref[...], preferred_element_type=F32)
    o_ref[...] = (acc * s_ref[...]).astype(o_ref.dtype)


def _matmul(x, w, out_dtype, col_scale=None, tm=MM_TM, tn=MM_TN):
    m, k = x.shape
    n = w.shape[1]
    tm, tn = min(tm, m), min(tn, n)
    in_specs = [pl.BlockSpec((tm, k), lambda j, i: (i, 0)), pl.BlockSpec((k, tn), lambda j, i: (0, j))]
    operands = [x, w]
    scratch = []
    if col_scale is not None:
        in_specs.append(pl.BlockSpec((1, tn), lambda j, i: (0, j)))
        operands.append(col_scale.reshape(1, n))
        scratch.append(pltpu.VMEM((k, tn), BF16))
    return pl.pallas_call(
        _matmul_kernel if col_scale is None else _scaled_matmul_kernel,
        grid=(n // tn, m // tm),
        in_specs=in_specs,
        out_specs=pl.BlockSpec((tm, tn), lambda j, i: (i, j)),
        out_shape=jax.ShapeDtypeStruct((m, n), out_dtype),
        scratch_shapes=scratch,
        name="matmul",
        compiler_params=_params("arbitrary", "arbitrary"),
    )(*operands)


def _matmul2_kernel(xa_ref, xb_ref, w_ref, o_ref, *, ka):
    acc = jnp.dot(xa_ref[...], w_ref[:ka, :], preferred_element_type=F32)
    acc = acc + jnp.dot(xb_ref[...], w_ref[ka:, :], preferred_element_type=F32)
    o_ref[...] = acc.astype(o_ref.dtype)


def _matmul2(xa, xb, w, out_dtype, tm=MM_TM, tn=MM_TN):
    m, ka = xa.shape
    kb = xb.shape[1]
    n = w.shape[1]
    tm, tn = min(tm, m), min(tn, n)
    return pl.pallas_call(
        functools.partial(_matmul2_kernel, ka=ka),
        grid=(n // tn, m // tm),
        in_specs=[pl.BlockSpec((tm, ka), lambda j, i: (i, 0)),
                  pl.BlockSpec((tm, kb), lambda j, i: (i, 0)),
                  pl.BlockSpec((ka + kb, tn), lambda j, i: (0, j))],
        out_specs=pl.BlockSpec((tm, tn), lambda j, i: (i, j)),
        out_shape=jax.ShapeDtypeStruct((m, n), out_dtype),
        name="matmul2",
        compiler_params=_params("parallel", "parallel"),
    )(xa, xb, w)


def _softmax_step(m_ref, a_ref, st_ref, p_ref, shift, vt):
    tk = st_ref.shape[0]
    strips = range(0, tk, SOFTMAX_STRIP)
    mx = None
    for r in strips:
        s = st_ref[r:r + SOFTMAX_STRIP, :]
        mx = s if mx is None else jnp.maximum(mx, s)
    m_prev = m_ref[...]
    m_next = jnp.maximum(m_prev, jnp.max(mx, axis=0, keepdims=True) - shift)
    alpha = jnp.exp2(m_prev - m_next)
    base = m_next + shift
    for r in strips:
        p_ref[r:r + SOFTMAX_STRIP, :] = jnp.exp2(st_ref[r:r + SOFTMAX_STRIP, :] - base).astype(p_ref.dtype)
    a_ref[...] = alpha * a_ref[...] + jnp.dot(vt, p_ref[...], preferred_element_type=F32)
    m_ref[...] = m_next


def _softmax_init(m_ref, a_ref):
    m_ref[...] = jnp.full(m_ref.shape, MASKED, F32)
    a_ref[...] = jnp.zeros(a_ref.shape, F32)


def _causal_tile_walk(i, scores, softmax):
    @pl.when(i == 0)
    def _():
        scores(i, 0, True)
        softmax(i, 0, True)

    @pl.when(i > 0)
    def _():
        scores(0, 0, False)
        pairs = (i - 1) // 2

        def body(jj, carry):
            a = 2 * jj
            scores(a + 1, 1, False)
            softmax(a, 0, False)
            scores(a + 2, 0, False)
            softmax(a + 1, 1, False)
            return carry

        lax.fori_loop(0, pairs, body, 0)
        a = 2 * pairs

        @pl.when(a == i - 1)
        def _():
            scores(i, 1, True)
            softmax(a, 0, False)
            softmax(i, 1, True)

        @pl.when(a == i - 2)
        def _():
            scores(a + 1, 1, False)
            softmax(a, 0, False)
            scores(i, 0, True)
            softmax(a + 1, 1, False)
            softmax(i, 0, True)


def _alibi_tiles(bias0, bias_diag, slope2, i):
    @pl.when(i == 0)
    def _():
        t = bias0.shape[0]
        rel = lax.broadcasted_iota(jnp.int32, (t, t), 1) - lax.broadcasted_iota(jnp.int32, (t, t), 0)
        bias = -slope2 * rel.astype(F32)
        bias0[...] = bias
        bias_diag[...] = jnp.where(rel >= 0, bias, MASKED)


def _softmax_result(a_ref):
    return a_ref[:LANES, :] / a_ref[LANES:LANES + 1, :]


def _cast_specs(weights, batch, nh, nq):
    first = nq // 2
    per_head = nq - first
    steps = batch * nh * per_head

    def block(b, h, i):
        return jnp.maximum((b * nh + h) * per_head + jnp.maximum(i - first, -1), 0), 0

    in_specs, out_specs, out_shapes = [], [], []
    for w in weights:
        rows, cols = w.shape
        blk = rows // steps
        assert blk * steps == rows and blk % BF16_SUBLANES == 0, (w.shape, steps)
        spec = pl.BlockSpec((blk, cols), block)
        in_specs.append(spec)
        out_specs.append(spec)
        out_shapes.append(jax.ShapeDtypeStruct(w.shape, BF16))
    return in_specs, out_specs, out_shapes, first


def _cast_blocks(src_refs, dst_refs, i, first):
    @pl.when(i >= first)
    def _():
        for src, dst in zip(src_refs, dst_refs):
            dst[...] = src[...].astype(dst.dtype)


def _diff_attn_kernel(*refs, t, lam_init, n_cast, cast_from):
    (slopes_ref, lq1_ref, lk1_ref, lq2_ref, lk2_ref, g_ref, q_ref, k_ref, v_ref), refs = refs[:9], refs[9:]
    cast_src, (o_ref,), cast_dst = refs[:n_cast], refs[n_cast:n_cast + 1], refs[n_cast + 1:2 * n_cast + 1]
    m1, a1, m2, a2, st1a, st1b, st2a, st2b, p1, p2, bias0, bias_diag, vt_ref = refs[2 * n_cast + 1:]
    st1, st2 = (st1a, st1b), (st2a, st2b)
    h = pl.program_id(1)
    i = pl.program_id(2)
    _cast_blocks(cast_src, cast_dst, i, cast_from)
    _transpose_values(v_ref, vt_ref, i)
    slope2 = slopes_ref[h] * LOG2E
    lam = (jnp.exp(jnp.sum(lq1_ref[...] * lk1_ref[...], keepdims=True))
           - jnp.exp(jnp.sum(lq2_ref[...] * lk2_ref[...], keepdims=True)) + lam_init)

    q = q_ref[...]
    lane = lax.broadcasted_iota(jnp.int32, q.shape, 1)
    q1 = jnp.where(lane < DIFF_QK_DIM, q, jnp.zeros_like(q))
    q2 = jnp.where(lane >= DIFF_QK_DIM, q, jnp.zeros_like(q))

    _alibi_tiles(bias0, bias_diag, slope2, i)

    _softmax_init(m1, a1)
    _softmax_init(m2, a2)

    def scores(j, slot, causal):
        kt = k_ref[pl.ds(pl.multiple_of(j * t, t), t), :]
        bias = bias_diag if causal else bias0
        st1[slot][...] = _nt_dot(kt, q1) + bias[...]
        st2[slot][...] = _nt_dot(kt, q2) + bias[...]

    def softmax(j, slot, causal):
        shift = 0.0 if causal else slope2 * ((i - j) * t).astype(F32)
        vt = vt_ref[j]
        _softmax_step(m1, a1, st1[slot], p1, shift, vt)
        _softmax_step(m2, a2, st2[slot], p2, shift, vt)

    _causal_tile_walk(i, scores, softmax)

    ot = _softmax_result(a1) - lam * _softmax_result(a2)
    ot = ot * lax.rsqrt(jnp.mean(ot * ot, axis=0, keepdims=True) + SUBLN_EPS)
    o_ref[...] = (ot.T * g_ref[...] * (1.0 - lam_init)).astype(o_ref.dtype)


def _transpose_values(v_ref, vt_ref, i):
    @pl.when(i == 0)
    def _():
        nq, rows, t = vt_ref.shape
        first_row = lax.broadcasted_iota(jnp.int32, (rows - LANES, t), 0) == 0
        tail = jnp.where(first_row, 1.0, 0.0).astype(vt_ref.dtype)
        for j in range(nq):
            vt_ref[j, :LANES, :] = v_ref[j * t:(j + 1) * t, :].astype(F32).T.astype(vt_ref.dtype)
            vt_ref[j, LANES:, :] = tail


def _diff_attention(proj, lq1, lk1, lq2, lk2, subln_g, slopes, batch, seq, lam_init, cast_weights):
    t = min(DIFF_T, seq)
    nq = seq // t
    nh = N_DIFF_HEADS
    vec = lambda n: pl.BlockSpec((1, n), lambda b, h, i: (0, 0))
    c_in, c_out, c_shapes, cast_from = _cast_specs(cast_weights, batch, nh, nq)
    kern = functools.partial(_diff_attn_kernel, t=t, lam_init=lam_init, n_cast=len(cast_weights),
                             cast_from=cast_from)
    stat = pltpu.VMEM((1, t), F32)
    acc = pltpu.VMEM((DV_ROWS, t), F32)
    out = pl.pallas_call(
        kern,
        grid=(batch, nh, nq),
        in_specs=[pl.BlockSpec(memory_space=pltpu.SMEM),
                  vec(DIFF_QK_DIM), vec(DIFF_QK_DIM), vec(DIFF_QK_DIM), vec(DIFF_QK_DIM), vec(LANES),
                  pl.BlockSpec((t, LANES), lambda b, h, i: (b * nq + i, h)),
                  pl.BlockSpec((seq, LANES), lambda b, h, i: (b, nh + h)),
                  pl.BlockSpec((seq, LANES), lambda b, h, i: (b, 2 * nh + h))] + c_in,
        out_specs=[pl.BlockSpec((t, LANES), lambda b, h, i: (b * nq + i, h))] + c_out,
        out_shape=[jax.ShapeDtypeStruct((batch * seq, nh * LANES), BF16)] + c_shapes,
        scratch_shapes=[stat, acc, stat, acc] + [pltpu.VMEM((t, t), F32)] * 4
                       + [pltpu.VMEM((t, t), BF16)] * 2 + [pltpu.VMEM((t, t), F32)] * 2
                       + [pltpu.VMEM((nq, DV_ROWS, t), BF16)],
        name="diff_attention",
        compiler_params=_params("arbitrary", "arbitrary", "arbitrary"),
    )(slopes, lq1.reshape(1, -1), lk1.reshape(1, -1), lq2.reshape(1, -1), lk2.reshape(1, -1),
      subln_g.reshape(1, -1), proj, proj, proj, *cast_weights)
    return out[0], out[1:]


def _moba_kernel(*refs, nb, t, n_cast, cast_from):
    (slopes_ref, q_ref, k_ref, v_ref), refs = refs[:4], refs[4:]
    cast_src, (o_ref,), cast_dst = refs[:n_cast], refs[n_cast:n_cast + 1], refs[n_cast + 1:2 * n_cast + 1]
    km_hi, km_mid, km_lo, selb, m_r, a_r, st_a, st_b, p, bias0, bias_diag, vt_ref = refs[2 * n_cast + 1:]
    st = (st_a, st_b)

    blk = MOBA_BLOCK
    per = t // blk
    h = pl.program_id(1)
    i = pl.program_id(2)
    slope2 = slopes_ref[h] * LOG2E
    _cast_blocks(cast_src, cast_dst, i, cast_from)
    _transpose_values(v_ref, vt_ref, i)

    @pl.when(i == 0)
    def _():
        km = jnp.sum(k_ref[...].astype(F32).reshape(nb, blk, LANES), axis=1) * (1.0 / blk)
        hi = km.astype(BF16)
        r1 = km - hi.astype(F32)
        mid = r1.astype(BF16)
        km_hi[...] = hi
        km_mid[...] = mid
        km_lo[...] = (r1 - mid.astype(F32)).astype(BF16)

    q = q_ref[...]
    gate = _nt_dot(km_hi[...], q) + _nt_dot(km_mid[...], q) + _nt_dot(km_lo[...], q)
    blk_id = lax.broadcasted_iota(jnp.int32, gate.shape, 0)
    q_sub = lax.broadcasted_iota(jnp.int32, (1, t), 1) // blk
    past = blk_id < i * per + q_sub
    gate = jnp.where(past, gate, NEG_INF)
    beaten = jnp.zeros(gate.shape, jnp.int32)
    for n in range(nb):
        gn = gate[n:n + 1, :]
        beaten = beaten + jnp.where(gn > gate, 1, jnp.where(gn == gate, jnp.where(blk_id > n, 1, 0), 0))
    selb[...] = jnp.where(past, jnp.where(beaten < MOBA_TOPK, 0.0, MASKED), MASKED)

    _alibi_tiles(bias0, bias_diag, slope2, i)
    _softmax_init(m_r, a_r)

    def scores(j, slot, own):
        rows = []
        for u in range(per):
            row = selb[pl.ds(j * per + u, 1), :]
            if own:
                row = jnp.where(q_sub <= u, 0.0, row)
            rows.append(jnp.broadcast_to(row, (blk, t)))
        bias = bias_diag if own else bias0
        st[slot][...] = (_nt_dot(k_ref[pl.ds(pl.multiple_of(j * t, t), t), :], q) + bias[...]
                         + jnp.concatenate(rows, axis=0))

    def softmax(j, slot, own):
        shift = 0.0 if own else slope2 * ((i - j) * t).astype(F32)
        _softmax_step(m_r, a_r, st[slot], p, shift, vt_ref[j])

    _causal_tile_walk(i, scores, softmax)
    o_ref[...] = _softmax_result(a_r).T.astype(o_ref.dtype)


def _moba_attention(proj, slopes, batch, seq, cast_weights):
    blk = MOBA_BLOCK
    t = min(MOBA_T, seq)
    nq = seq // t
    nb = seq // blk
    nh = N_MOBA_HEADS
    base = 3 * N_DIFF_HEADS
    c_in, c_out, c_shapes, cast_from = _cast_specs(cast_weights, batch, nh, nq)
    kern = functools.partial(_moba_kernel, nb=nb, t=t, n_cast=len(cast_weights), cast_from=cast_from)
    out = pl.pallas_call(
        kern,
        grid=(batch, nh, nq),
        in_specs=[pl.BlockSpec(memory_space=pltpu.SMEM),
                  pl.BlockSpec((t, LANES), lambda b, h, i: (b * nq + i, base + h)),
                  pl.BlockSpec((seq, LANES), lambda b, h, i: (b, base + nh + h)),
                  pl.BlockSpec((seq, LANES), lambda b, h, i: (b, base + 2 * nh + h))] + c_in,
        out_specs=[pl.BlockSpec((t, LANES), lambda b, h, i: (b * nq + i, h))] + c_out,
        out_shape=[jax.ShapeDtypeStruct((batch * seq, nh * LANES), BF16)] + c_shapes,
        scratch_shapes=[pltpu.VMEM((nb, LANES), BF16), pltpu.VMEM((nb, LANES), BF16), pltpu.VMEM((nb, LANES), BF16),
                        pltpu.VMEM((nb, t), F32),
                        pltpu.VMEM((1, t), F32), pltpu.VMEM((DV_ROWS, t), F32),
                        pltpu.VMEM((t, t), F32), pltpu.VMEM((t, t), F32), pltpu.VMEM((t, t), BF16),
                        pltpu.VMEM((t, t), F32), pltpu.VMEM((t, t), F32),
                        pltpu.VMEM((nq, DV_ROWS, t), BF16)],
        name="moba_attention",
        compiler_params=_params("arbitrary", "arbitrary", "arbitrary"),
    )(slopes, proj, proj, proj, *cast_weights)
    return out[0], out[1:]


def _swa_kernel(slopes_ref, sinks_ref, q_ref, kp_ref, kc_ref, vp_ref, vc_ref, o_ref, bias_ref, st_ref, p_ref,
                *, group):
    w = SWA_WINDOW
    b = pl.program_id(0)
    n = pl.program_id(1)
    nq = group * w
    gw = group * SWA_HEAD_DIM
    lane_head = lax.broadcasted_iota(jnp.int32, (1, nq), 1) // w
    low = lax.broadcasted_iota(jnp.int32, (w, LANES), 1) < SWA_HEAD_DIM
    upper = lax.broadcasted_iota(jnp.int32, (LANES, w), 0) < SWA_HEAD_DIM
    first = jnp.where(n > 0, 0.0, MASKED)
    strips = range(0, 2 * w, SOFTMAX_STRIP)

    def per_head(values_ref, g, scale):
        row = jnp.zeros((1, nq), F32)
        for hh in range(group):
            row = jnp.where(lane_head == hh, values_ref[g * group + hh] * scale, row)
        return row

    for gi in range(SWA_KV_PER_STEP):
        g = pl.program_id(2) * SWA_KV_PER_STEP + gi

        @pl.when(jnp.logical_and(b == 0, n == 0))
        def _():
            c = lax.broadcasted_iota(jnp.int32, (2 * w, nq), 0)
            r = lax.broadcasted_iota(jnp.int32, (2 * w, nq), 1) % w
            dist = r - c + w
            ok = jnp.logical_and(dist >= 0, dist < w)
            bias_ref[g] = jnp.where(ok, -per_head(slopes_ref, g, LOG2E) * dist.astype(F32), MASKED)

        qz = []
        for pair in range(group // 2):
            qp = q_ref[:, gi * gw + pair * LANES:gi * gw + (pair + 1) * LANES]
            qz += [jnp.where(low, qp, jnp.zeros_like(qp)), jnp.where(low, jnp.zeros_like(qp), qp)]
        qz = jnp.concatenate(qz, axis=0)
        keys = slice(gi * LANES, (gi + 1) * LANES)
        st_ref[gi, :w, :] = _nt_dot(kp_ref[:, keys], qz) + bias_ref[g, :w, :] + first
        st_ref[gi, w:, :] = _nt_dot(kc_ref[:, keys], qz) + bias_ref[g, w:, :]

    for gi in range(SWA_KV_PER_STEP):
        g = pl.program_id(2) * SWA_KV_PER_STEP + gi
        mx = None
        for r0 in strips:
            s = st_ref[gi, r0:r0 + SOFTMAX_STRIP, :]
            mx = s if mx is None else jnp.maximum(mx, s)
        sink2 = per_head(sinks_ref, g, LOG2E)
        m = jnp.maximum(jnp.max(mx, axis=0, keepdims=True), sink2)
        for r0 in strips:
            p_ref[gi, r0:r0 + SOFTMAX_STRIP, :] = jnp.exp2(st_ref[gi, r0:r0 + SOFTMAX_STRIP, :] - m).astype(p_ref.dtype)
        vt = jnp.concatenate([vp_ref[gi], vc_ref[gi]], axis=1)
        acc = jnp.dot(vt, p_ref[gi], preferred_element_type=F32)
        ot = acc[:LANES, :] / (acc[LANES:LANES + 1, :] + jnp.exp2(sink2 - m))
        for pair in range(group // 2):
            even = ot[:, (2 * pair) * w:(2 * pair + 1) * w]
            odd = ot[:, (2 * pair + 1) * w:(2 * pair + 2) * w]
            o_ref[:, gi * gw + pair * LANES:gi * gw + (pair + 1) * LANES] = (
                jnp.where(upper, even, odd).T.astype(o_ref.dtype))


def _swa_attention(proj, k2, v2t, sinks, slopes, batch, seq):
    w = SWA_WINDOW
    nblk = seq // w
    group = N_SWA_HEADS // N_SWA_KV_HEADS
    per = SWA_KV_PER_STEP
    gw = per * group * SWA_HEAD_DIM
    prev = lambda b, n, g: (b * nblk + jnp.maximum(n - 1, 0), g)
    cur = lambda b, n, g: (b * nblk + n, g)
    vblock = (None, None, per, DV_ROWS, w)
    return pl.pallas_call(
        functools.partial(_swa_kernel, group=group),
        grid=(batch, nblk, N_SWA_KV_HEADS // per),
        in_specs=[pl.BlockSpec(memory_space=pltpu.SMEM), pl.BlockSpec(memory_space=pltpu.SMEM),
                  pl.BlockSpec((w, gw), cur),
                  pl.BlockSpec((w, per * LANES), prev), pl.BlockSpec((w, per * LANES), cur),
                  pl.BlockSpec(vblock, lambda b, n, g: (b, jnp.maximum(n - 1, 0), g, 0, 0)),
                  pl.BlockSpec(vblock, lambda b, n, g: (b, n, g, 0, 0))],
        out_specs=pl.BlockSpec((w, gw), cur),
        out_shape=jax.ShapeDtypeStruct((batch * seq, N_SWA_HEADS * SWA_HEAD_DIM), BF16),
        scratch_shapes=[pltpu.VMEM((N_SWA_KV_HEADS, 2 * w, group * w), F32),
                        pltpu.VMEM((per, 2 * w, group * w), F32), pltpu.VMEM((per, 2 * w, group * w), BF16)],
        name="swa_attention",
        compiler_params=_params("arbitrary", "arbitrary", "arbitrary"),
    )(slopes, sinks, proj, k2, k2, v2t, v2t)


def _layer_norm(z, g, b):
    mu = jnp.mean(z, axis=-1, keepdims=True)
    zc = z - mu
    var = jnp.mean(zc * zc, axis=-1, keepdims=True)
    return zc * lax.rsqrt(var + LN_EPS) * g + b


def _split3(a):
    hi = a.astype(BF16)
    r1 = a - hi.astype(F32)
    mid = r1.astype(BF16)
    return hi, mid, (r1 - mid.astype(F32)).astype(BF16)


def _norm_router_kernel(x_ref, h_ref, g_ref, b_ref, rw_ref, rb_ref,
                        y_ref, eid_ref, w_ref, pos_ref, cnt_ref, carry, *, tm):
    step = pl.program_id(0)

    @pl.when(step == 0)
    def _():
        carry[...] = jnp.zeros(carry.shape, F32)

    y = _layer_norm(DEEPNORM_ALPHA * x_ref[...] + h_ref[...], g_ref[...], b_ref[...])
    y_ref[...] = y

    ne = rb_ref.shape[1]
    yh, ym, yl = _split3(y)
    dot = lambda a, b: jnp.dot(a, b, preferred_element_type=F32)
    by_hi = dot(yh, rw_ref[...])
    by_mid = dot(ym, rw_ref[:, :2 * ne])
    by_lo = dot(yl, rw_ref[:, :ne])
    logits = ((by_lo + by_mid[:, ne:] + by_hi[:, 2 * ne:]) + (by_mid[:, :ne] + by_hi[:, ne:2 * ne])
              + by_hi[:, :ne] + rb_ref[...])

    e_id = lax.broadcasted_iota(jnp.int32, logits.shape, 1)
    beaten = jnp.zeros(logits.shape, jnp.int32)
    for n in range(N_EXPERTS):
        ln = logits[:, n:n + 1]
        beaten = beaten + jnp.where(ln > logits, 1, jnp.where(ln == logits, jnp.where(e_id > n, 1, 0), 0))
    sel = beaten < TOP_K
    ex = jnp.where(sel, jnp.exp(logits - jnp.max(logits, axis=1, keepdims=True)), 0.0)
    weight = ex / jnp.sum(ex, axis=1, keepdims=True)

    self = jnp.where(sel, 1.0, 0.0)
    rr = lax.broadcasted_iota(jnp.int32, (tm, tm), 0)
    cc = lax.broadcasted_iota(jnp.int32, (tm, tm), 1)
    before = jnp.where(rr > cc, 1.0, 0.0).astype(BF16)
    pos = carry[...] + jnp.dot(before, self.astype(BF16), preferred_element_type=F32)
    carry[...] = carry[...] + jnp.sum(self, axis=0, keepdims=True)
    cnt_ref[...] = carry[...].astype(jnp.int32)

    e_f = e_id.astype(F32)
    k_id = lax.broadcasted_iota(jnp.int32, (tm, TOP_K), 1)
    e_k = jnp.zeros((tm, TOP_K), F32)
    w_k = jnp.zeros((tm, TOP_K), F32)
    p_k = jnp.zeros((tm, TOP_K), F32)
    for k in range(TOP_K):
        hit = beaten == k
        pick = lambda a: jnp.sum(jnp.where(hit, a, 0.0), axis=1, keepdims=True)
        e_k = jnp.where(k_id == k, pick(e_f), e_k)
        w_k = jnp.where(k_id == k, pick(weight), w_k)
        p_k = jnp.where(k_id == k, pick(pos), p_k)
    eid_ref[...] = e_k.astype(jnp.int32)
    w_ref[...] = w_k
    pos_ref[...] = p_k.astype(jnp.int32)


def _norm_router(x, h, g, b, rw, rb):
    t, d = x.shape
    tm = min(NORM_TM, t)
    ne = rw.shape[1]
    row = pl.BlockSpec((tm, d), lambda i: (i, 0))
    vec = pl.BlockSpec((1, d), lambda i: (0, 0))
    per_k = pl.BlockSpec((tm, TOP_K), lambda i: (i, 0))
    return pl.pallas_call(
        functools.partial(_norm_router_kernel, tm=tm),
        grid=(t // tm,),
        in_specs=[row, row, vec, vec, pl.BlockSpec((d, 3 * ne), lambda i: (0, 0)),
                  pl.BlockSpec((1, ne), lambda i: (0, 0))],
        out_specs=[row, per_k, per_k, per_k, pl.BlockSpec((1, ne), lambda i: (0, 0))],
        out_shape=[jax.ShapeDtypeStruct((t, d), F32), jax.ShapeDtypeStruct((t, TOP_K), jnp.int32),
                   jax.ShapeDtypeStruct((t, TOP_K), F32), jax.ShapeDtypeStruct((t, TOP_K), jnp.int32),
                   jax.ShapeDtypeStruct((1, ne), jnp.int32)],
        scratch_shapes=[pltpu.VMEM((1, ne), F32)],
        name="norm_router",
        compiler_params=_params("arbitrary"),
    )(x, h, g.reshape(1, d), b.reshape(1, d), jnp.concatenate(_split3(rw), axis=1), rb.reshape(1, ne))


def _row_copy(src_ref, dst_ref, src_row, dst_row, sem):
    return pltpu.make_async_copy(src_ref.at[pl.ds(src_row, 1), :], dst_ref.at[pl.ds(dst_row, 1), :], sem)


def _dispatch_kernel(slot_ref, pad_lo_ref, pad_hi_ref, nu_ref, y_ref, xs_hbm, zrow, sem, pad_sem, *, tm, ne):
    step = pl.program_id(0)

    @pl.when(step == 0)
    def _():
        zrow[...] = jnp.zeros(zrow.shape, zrow.dtype)
        tile_rows = zrow.shape[0]
        tail = lambda tl: pltpu.make_async_copy(
            zrow, xs_hbm.at[pl.ds(pl.multiple_of(tl * tile_rows, tile_rows), tile_rows), :], pad_sem)

        def fill_tile(tl, carry):
            tail(tl).start()
            return carry

        def tile_filled(tl, carry):
            tail(tl).wait()
            return carry

        lax.fori_loop(nu_ref[0], xs_hbm.shape[0] // tile_rows, fill_tile, 0)
        lax.fori_loop(nu_ref[0], xs_hbm.shape[0] // tile_rows, tile_filled, 0)
        for e in range(ne):
            lo = pad_lo_ref[e]
            hi = pad_hi_ref[e]

            def fill(r, carry):
                _row_copy(zrow, xs_hbm, 0, r, pad_sem).start()
                return carry

            def filled(r, carry):
                _row_copy(zrow, xs_hbm, 0, r, pad_sem).wait()
                return carry

            lax.fori_loop(lo, hi, fill, 0)
            lax.fori_loop(lo, hi, filled, 0)

    base = step * tm * TOP_K

    def issue(r, carry):
        for k in range(TOP_K):
            _row_copy(y_ref, xs_hbm, r, slot_ref[base + r * TOP_K + k], sem).start()
        return carry

    def drain(r, carry):
        for k in range(TOP_K):
            _row_copy(y_ref, xs_hbm, 0, 0, sem).wait()
        return carry

    lax.fori_loop(0, tm, issue, 0, unroll=4)
    lax.fori_loop(0, tm, drain, 0, unroll=16)


def _dispatch_rows(y, slots, pad_lo, pad_hi, n_used, n_rows):
    t, d = y.shape
    tm = min(DISPATCH_T, t)
    ne = pad_lo.shape[0]
    return pl.pallas_call(
        functools.partial(_dispatch_kernel, tm=tm, ne=ne),
        grid_spec=pltpu.PrefetchScalarGridSpec(
            num_scalar_prefetch=4,
            grid=(t // tm,),
            in_specs=[pl.BlockSpec((tm, d), lambda i, s, lo, hi, nu: (i, 0))],
            out_specs=pl.BlockSpec(memory_space=pl.ANY),
            scratch_shapes=[pltpu.VMEM((MOE_TM, d), y.dtype), pltpu.SemaphoreType.DMA(()),
                            pltpu.SemaphoreType.DMA(())]),
        out_shape=jax.ShapeDtypeStruct((n_rows, d), y.dtype),
        name="dispatch_scatter",
        compiler_params=_params("arbitrary"),
    )(slots, pad_lo, pad_hi, n_used, y)


def _gate_up_kernel(te_ref, nu_ref, x_ref, wg_ref, wu_ref, bg_ref, bu_ref, h_ref):
    t = pl.program_id(0)

    @pl.when(t < nu_ref[0])
    def _():
        x = x_ref[...].astype(BF16)
        gate = jnp.minimum(jnp.dot(x, wg_ref[0], preferred_element_type=F32) + bg_ref[0], SWIGLU_LIMIT)
        up = jnp.clip(jnp.dot(x, wu_ref[0], preferred_element_type=F32) + bu_ref[0], -SWIGLU_LIMIT, SWIGLU_LIMIT)
        h_ref[...] = ((up + 1.0) * gate * jax.nn.sigmoid(SWIGLU_ALPHA * gate)).astype(h_ref.dtype)

    @pl.when(t >= nu_ref[0])
    def _():
        h_ref[...] = jnp.zeros(h_ref.shape, h_ref.dtype)


def _down_kernel(te_ref, nu_ref, h_ref, wd_ref, bd_ref, y_ref, wd_bf):
    t = pl.program_id(0)

    @pl.when(jnp.logical_or(t == 0, te_ref[t] != te_ref[jnp.maximum(t - 1, 0)]))
    def _():
        wd_bf[...] = wd_ref[0].astype(BF16)

    @pl.when(t < nu_ref[0])
    def _():
        y_ref[...] = jnp.dot(h_ref[...], wd_bf[...], preferred_element_type=F32) + bd_ref[0]

    @pl.when(t >= nu_ref[0])
    def _():
        y_ref[...] = jnp.zeros(y_ref.shape, y_ref.dtype)


def _expert_ffn(xs, tile_expert, n_used, w_gate, b_gate, w_up, b_up, w_down, b_down):
    p, d = xs.shape
    ne, _, f = w_gate.shape
    tm = MOE_TM
    nt = p // tm
    used = lambda t, nu: jnp.minimum(t, nu[0] - 1)
    expert = lambda t, te, nu: (te[t], 0, 0)
    h = pl.pallas_call(
        _gate_up_kernel,
        grid_spec=pltpu.PrefetchScalarGridSpec(
            num_scalar_prefetch=2,
            grid=(nt,),
            in_specs=[pl.BlockSpec((tm, d), lambda t, te, nu: (used(t, nu), 0)),
                      pl.BlockSpec((1, d, f), expert), pl.BlockSpec((1, d, f), expert),
                      pl.BlockSpec((1, 1, f), expert), pl.BlockSpec((1, 1, f), expert)],
            out_specs=pl.BlockSpec((tm, f), lambda t, te, nu: (t, 0))),
        out_shape=jax.ShapeDtypeStruct((p, f), BF16),
        name="expert_gate_up",
        compiler_params=_params("arbitrary"),
    )(tile_expert, n_used, xs, w_gate, w_up, b_gate.reshape(ne, 1, f), b_up.reshape(ne, 1, f))
    return pl.pallas_call(
        _down_kernel,
        grid_spec=pltpu.PrefetchScalarGridSpec(
            num_scalar_prefetch=2,
            grid=(nt,),
            in_specs=[pl.BlockSpec((tm, f), lambda t, te, nu: (used(t, nu), 0)),
                      pl.BlockSpec((1, f, d), expert), pl.BlockSpec((1, 1, d), expert)],
            out_specs=pl.BlockSpec((tm, d), lambda t, te, nu: (t, 0)),
            scratch_shapes=[pltpu.VMEM((f, d), BF16)]),
        out_shape=jax.ShapeDtypeStruct((p, d), F32),
        name="expert_down",
        compiler_params=_params("arbitrary"),
    )(tile_expert, n_used, h, w_down, b_down.reshape(ne, 1, d))


def _combine_norm_kernel(slot_ref, x_ref, w_ref, g_ref, b_ref, ys_hbm, o_ref, obf_ref, buf, sems, *, tc):
    step = pl.program_id(0)
    cur = step % 2

    def fetch(s, half):
        base = s * tc * TOP_K

        def issue(r, carry):
            for k in range(TOP_K):
                _row_copy(ys_hbm, buf.at[half, k], slot_ref[base + r * TOP_K + k], r, sems.at[half]).start()
            return carry

        lax.fori_loop(0, tc, issue, 0, unroll=4)

    @pl.when(step == 0)
    def _():
        fetch(0, 0)

    @pl.when(step + 1 < pl.num_programs(0))
    def _():
        fetch(step + 1, 1 - cur)

    def drain(r, carry):
        for k in range(TOP_K):
            _row_copy(ys_hbm, buf.at[cur, 0], 0, 0, sems.at[cur]).wait()
        return carry

    lax.fori_loop(0, tc, drain, 0, unroll=16)

    w = w_ref[...]
    moe = w[:, 0:1] * buf[cur, 0]
    for k in range(1, TOP_K):
        moe = moe + w[:, k:k + 1] * buf[cur, k]
    out = _layer_norm(DEEPNORM_ALPHA * x_ref[...] + moe, g_ref[...], b_ref[...])
    o_ref[...] = out
    obf_ref[...] = out.astype(obf_ref.dtype)


def _combine_norm(x, ys, slots, w4, g, b):
    t, d = x.shape
    tc = min(COMBINE_T, t)
    row = pl.BlockSpec((tc, d), lambda i, s: (i, 0))
    return pl.pallas_call(
        functools.partial(_combine_norm_kernel, tc=tc),
        grid_spec=pltpu.PrefetchScalarGridSpec(
            num_scalar_prefetch=1,
            grid=(t // tc,),
            in_specs=[row,
                      pl.BlockSpec((tc, TOP_K), lambda i, s: (i, 0)),
                      pl.BlockSpec((1, d), lambda i, s: (0, 0)),
                      pl.BlockSpec((1, d), lambda i, s: (0, 0)),
                      pl.BlockSpec(memory_space=pl.ANY)],
            out_specs=[row, row],
            scratch_shapes=[pltpu.VMEM((2, TOP_K, tc, d), F32), pltpu.SemaphoreType.DMA((2,))]),
        out_shape=[jax.ShapeDtypeStruct((t, d), F32), jax.ShapeDtypeStruct((t, d), BF16)],
        name="combine_norm",
        compiler_params=_params("arbitrary"),
    )(slots, x, w4, g.reshape(1, d), b.reshape(1, d), ys)


def _routing_tables(eid, pos, counts, tm):
    t = eid.shape[0]
    counts = counts[0]
    ne = counts.shape[0]
    padded = ((counts + tm - 1) // tm) * tm
    ends = jnp.cumsum(padded)
    starts = ends - padded
    n_tiles = (t * TOP_K) // tm + ne
    n_used = (ends[-1] // tm).astype(jnp.int32)
    tile_start = jnp.arange(n_tiles, dtype=jnp.int32) * tm
    tile_expert = jnp.sum((tile_start[:, None] >= ends[None, :]).astype(jnp.int32), axis=1)
    tile_expert = jnp.minimum(tile_expert, ne - 1)
    last = jnp.sum(jnp.where(jnp.arange(n_tiles) == n_used - 1, tile_expert, 0))
    tile_expert = jnp.where(jnp.arange(n_tiles) < n_used, tile_expert, last).astype(jnp.int32)
    group_start = jnp.sum(jnp.where(eid[:, :, None] == jnp.arange(ne)[None, None, :], starts[None, None, :], 0), axis=2)
    slots = (group_start + pos).astype(jnp.int32).reshape(-1)
    return slots, tile_expert, n_used.reshape(1), (starts + counts).astype(jnp.int32), ends.astype(jnp.int32), n_tiles * tm


def _post_norm_moe(x, h, ln1_g, ln1_b, router_w, router_b, w_gate, b_gate, w_up, b_up, w_down, b_down,
                   ln2_g, ln2_b):
    y, eid, w4, pos, counts = _norm_router(x, h, ln1_g, ln1_b, router_w, router_b)
    slots, tile_expert, n_used, pad_lo, pad_hi, n_rows = _routing_tables(eid, pos, counts, MOE_TM)
    xs = _dispatch_rows(y, slots, pad_lo, pad_hi, n_used, n_rows)
    ys = _expert_ffn(xs, tile_expert, n_used, w_gate, b_gate, w_up, b_up, w_down, b_down)
    return _combine_norm(y, ys, slots, w4, ln2_g, ln2_b)


def kernel(x, l0_w_in, l0_lambda_q1, l0_lambda_k1, l0_lambda_q2, l0_lambda_k2, l0_subln_g, l0_w_o, l0_ln1_g, l0_ln1_b, l0_router_w, l0_router_b, l0_w_gate, l0_b_gate, l0_w_up, l0_b_up, l0_w_down, l0_b_down, l0_ln2_g, l0_ln2_b, l1_w_in, l1_sinks, l1_w_o, l1_ln1_g, l1_ln1_b, l1_router_w, l1_router_b, l1_w_gate, l1_b_gate, l1_w_up, l1_b_up, l1_w_down, l1_b_down, l1_ln2_g, l1_ln2_b):
    batch, seq, d = x.shape
    xt = x.reshape(batch * seq, d)
    flat = lambda w: w.reshape(-1, w.shape[-1])
    like = lambda wb, w: wb.reshape(w.shape)

    a_w = N_DIFF_HEADS * 2 * DIFF_QK_DIM
    b_w = N_MOBA_HEADS * LANES
    one = lambda n: jnp.ones((n,), F32)
    scale0 = jnp.concatenate([one(a_w) * (LOG2E * DIFF_QK_DIM ** -0.5), one(2 * a_w),
                              one(b_w) * (LOG2E * LANES ** -0.5), one(2 * b_w)])
    proj = _matmul(xt.astype(BF16), l0_w_in, BF16, col_scale=scale0)
    slopes = _alibi_slopes(N_DIFF_HEADS + N_MOBA_HEADS)
    lam_init = 0.8 - 0.6 * math.exp(-0.3 * 0)
    attn_a, (wg0, wu0, wo0) = _diff_attention(
        proj, l0_lambda_q1, l0_lambda_k1, l0_lambda_q2, l0_lambda_k2, l0_subln_g, slopes[0::2], batch, seq,
        lam_init, [flat(l0_w_gate), flat(l0_w_up), l0_w_o])
    attn_b, (wg1, wu1, wo1) = _moba_attention(
        proj, slopes[1::2], batch, seq, [flat(l1_w_gate), flat(l1_w_up), l1_w_o])
    h = _matmul2(attn_a, attn_b, wo0, F32)
    xt, xt_bf = _post_norm_moe(xt, h, l0_ln1_g, l0_ln1_b, l0_router_w, l0_router_b, like(wg0, l0_w_gate), l0_b_gate,
                               like(wu0, l0_w_up), l0_b_up, l0_w_down, l0_b_down, l0_ln2_g, l0_ln2_b)

    q_w = N_SWA_HEADS * SWA_HEAD_DIM
    kv_w = N_SWA_KV_HEADS * SWA_HEAD_DIM
    scale1 = jnp.concatenate([one(q_w) * (LOG2E * SWA_HEAD_DIM ** -0.5), one(2 * kv_w)])
    proj = _matmul(xt_bf, l1_w_in, BF16, col_scale=scale1)
    dup = lambda a: jnp.broadcast_to(a.reshape(-1, N_SWA_KV_HEADS, 1, SWA_HEAD_DIM),
                                     (a.shape[0], N_SWA_KV_HEADS, 2, SWA_HEAD_DIM)).reshape(a.shape[0], 2 * kv_w)
    k2 = dup(proj[:, q_w:q_w + kv_w])
    vt = proj[:, q_w + kv_w:].reshape(batch, seq // SWA_WINDOW, SWA_WINDOW, N_SWA_KV_HEADS, SWA_HEAD_DIM)
    vt = vt.transpose(0, 1, 3, 4, 2)
    fill = lambda rows, value: jnp.full(vt.shape[:3] + (rows, SWA_WINDOW), value, vt.dtype)
    v2t = jnp.concatenate([vt, vt, fill(1, 1.0), fill(DV_ROWS - LANES - 1, 0.0)], axis=3)
    attn = _swa_attention(proj, k2, v2t, l1_sinks, _alibi_slopes(N_SWA_HEADS), batch, seq)
    h = _matmul(attn, wo1, F32)
    xt, _ = _post_norm_moe(xt, h, l1_ln1_g, l1_ln1_b, l1_router_w, l1_router_b, like(wg1, l1_w_gate), l1_b_gate,
                           like(wu1, l1_w_up), l1_b_up, l1_w_down, l1_b_down, l1_ln2_g, l1_ln2_b)
    return xt.reshape(batch, seq, d)
```
